```python
import math
import jax, jax.numpy as jnp
from jax import lax
import numpy as np

D_MODEL = 1024
BATCH = 2
SEQ = 8192
DEPTH = 4
DEC_BATCH = 32
DEC_SEQ = 4
PAST_LEN = 8192
PAGE_SIZE = 128

N_MIXERS = 4
N_CONV_A = (DEPTH + 3) // N_MIXERS
N_CONV_B = (DEPTH + 2) // N_MIXERS
N_SB = (DEPTH + 1) // N_MIXERS
N_RET = DEPTH // N_MIXERS

CONV_A_WIDTH = 31
CONV_B_WIDTH = 3
SB_HEADS = 16
SB_HEAD_DIM = D_MODEL // SB_HEADS
SB_SCALE = SB_HEAD_DIM ** -0.5
Q_BLOCK = 128
RET_DK = 256
RET_HEADS = D_MODEL // RET_DK
RET_DV = 2 * RET_DK
RET_CHUNK = 128
ROPE_BASE = 10000.0
MEM_TOKENS = 256
MEM_HEADS = 4
MEM_HEAD_DIM = D_MODEL // MEM_HEADS
MEM_SCALE = MEM_HEAD_DIM ** -0.5
D_FF = 4 * D_MODEL
NORM_EPS = 1e-6

kernel_name = 'hybrid_conformer_shortconv_stickbreak_retention_step'


def _rmsnorm(x, g):
    xf = x.astype(jnp.float32)
    y = xf * lax.rsqrt(jnp.mean(xf * xf, axis=-1, keepdims=True) + NORM_EPS)
    return (y * g.astype(jnp.float32)).astype(x.dtype)


def _layernorm(x, g, b=None):
    xf = x.astype(jnp.float32)
    xc = xf - jnp.mean(xf, axis=-1, keepdims=True)
    y = xc * lax.rsqrt(jnp.mean(xc * xc, axis=-1, keepdims=True) + NORM_EPS) * g.astype(jnp.float32)
    if b is not None:
        y = y + b.astype(jnp.float32)
    return y.astype(x.dtype)


def _causal_dwconv(xpad, w):
    return lax.conv_general_dilated(xpad, w[:, None, :].astype(xpad.dtype), window_strides=(1,), padding='VALID',
                                    dimension_numbers=('NWC', 'WIO', 'NWC'), feature_group_count=xpad.shape[-1])


def _conformer_conv(h, buf, w1, b1, wdw, bdw, ln_g, ln_b, w2):
    a, gate = jnp.split(h @ w1 + b1, 2, axis=-1)
    g = a * jax.nn.sigmoid(gate)
    xpad = jnp.concatenate([buf.astype(g.dtype), g], axis=1)
    c = _causal_dwconv(xpad, wdw) + bdw
    s = jax.nn.silu(_layernorm(c, ln_g, ln_b))
    return s @ w2, xpad[:, -(CONV_A_WIDTH - 1):]


def _short_conv(h, buf, w_in, w_conv, w_out):
    b_gate, c_gate, u = jnp.split(h @ w_in, 3, axis=-1)
    cu = c_gate * u
    xpad = jnp.concatenate([buf.astype(cu.dtype), cu], axis=1)
    y = b_gate * _causal_dwconv(xpad, w_conv)
    return y @ w_out, xpad[:, -(CONV_B_WIDTH - 1):]


def _sb_weights(z, q_pos):
    mask = jnp.arange(z.shape[-1])[None, :] < q_pos[:, None]
    log_keep = jnp.where(mask, jax.nn.log_sigmoid(-z), 0.0)
    log_after = lax.cumsum(log_keep, axis=3, reverse=True) - log_keep
    return jnp.where(mask, jnp.exp(jax.nn.log_sigmoid(z) + log_after), 0.0)


def _sb_prompt(q, k, v, bias):
    b, t, h, d = q.shape
    kf, vf = k.astype(jnp.float32), v.astype(jnp.float32)
    bias = bias.astype(jnp.float32)[None, :, None, None]

    def block(i):
        start = i * Q_BLOCK
        qb = lax.dynamic_slice_in_dim(q, start, Q_BLOCK, axis=1).astype(jnp.float32)
        z = jnp.einsum('bqhd,bkhd->bhqk', qb, kf) * SB_SCALE + bias
        w = _sb_weights(z, start + jnp.arange(Q_BLOCK))
        return jnp.einsum('bhqk,bkhd->bqhd', w, vf)

    o = lax.map(block, jnp.arange(t // Q_BLOCK))
    return jnp.moveaxis(o, 0, 1).reshape(b, t, h, d).astype(q.dtype)


def _sb_sample(q, k, v, past_k, past_v, pos0, bias):
    qf = q.astype(jnp.float32)
    n_past = past_k.shape[1]
    z = jnp.concatenate([jnp.einsum('bqhd,bkhd->bhqk', qf, past_k.astype(jnp.float32)),
                         jnp.einsum('bqhd,bkhd->bhqk', qf, k.astype(jnp.float32))], axis=-1) * SB_SCALE
    z = z + bias.astype(jnp.float32)[None, :, None, None]
    w = _sb_weights(z, pos0 + jnp.arange(q.shape[1]))
    o = (jnp.einsum('bhqk,bkhd->bqhd', w[..., :n_past], past_v.astype(jnp.float32))
         + jnp.einsum('bhqk,bkhd->bqhd', w[..., n_past:], v.astype(jnp.float32)))
    return o.astype(q.dtype)


def _rotary(x, pos0):
    t, d = x.shape[1], x.shape[-1]
    inv_freq = jnp.power(ROPE_BASE, -jnp.arange(0, d, 2, dtype=jnp.float32) / d)
    ang = (pos0 + jnp.arange(t, dtype=jnp.float32))[:, None] * inv_freq[None, :]
    cos, sin = jnp.cos(ang)[None, :, None, :], jnp.sin(ang)[None, :, None, :]
    xf = x.astype(jnp.float32)
    x1, x2 = xf[..., : d // 2], xf[..., d // 2:]
    return jnp.concatenate([x1 * cos - x2 * sin, x1 * sin + x2 * cos], axis=-1).astype(x.dtype)


def _retention(q, k, v, s0):
    b, t, h, dk = q.shape
    dv = v.shape[-1]
    c = math.gcd(t, RET_CHUNK)
    n = t // c
    log_gamma = jnp.log(1.0 - jnp.power(2.0, -5.0 - jnp.arange(h, dtype=jnp.float32)))
    idx = jnp.arange(c, dtype=jnp.float32)
    diff = idx[:, None] - idx[None, :]
    intra = jnp.where(diff >= 0, jnp.exp(jnp.maximum(diff, 0.0)[None] * log_gamma[:, None, None]), 0.0)
    q_decay = jnp.exp((idx + 1.0)[:, None] * log_gamma[None, :])
    k_decay = jnp.exp((c - 1.0 - idx)[:, None] * log_gamma[None, :])
    chunk_decay = jnp.exp(c * log_gamma)

    def to_chunks(u):
        return jnp.swapaxes(u.astype(jnp.float32).reshape(b, n, c, h, u.shape[-1]), 0, 1)

    def step(state, inp):
        qc, kc, vc = inp
        scores = jnp.einsum('bihd,bjhd->bhij', qc, kc) * intra
        o = (jnp.einsum('bhij,bjhe->bihe', scores, vc)
             + jnp.einsum('bihd,bhde->bihe', qc, state) * q_decay[None, :, :, None])
        state = (state * chunk_decay[None, :, None, None]
                 + jnp.einsum('bjhd,bjhe->bhde', kc * k_decay[None, :, :, None], vc))
        return state, o

    s, o = lax.scan(step, s0.astype(jnp.float32), (to_chunks(q), to_chunks(k), to_chunks(v)))
    return jnp.swapaxes(o, 0, 1).reshape(b, t, h, dv), s


def _mem_kv(mem, g_mem, wk, wv, knorm):
    b, m, _ = mem.shape
    mn = _rmsnorm(mem, g_mem)
    k = _rmsnorm((mn @ wk).reshape(b, m, MEM_HEADS, MEM_HEAD_DIM), knorm)
    v = (mn @ wv).reshape(b, m, MEM_HEADS, MEM_HEAD_DIM)
    return k, v


def _mem_cross(h, mk, mv, wq, wo, qnorm):
    b, t, _ = h.shape
    q = _rmsnorm((h @ wq).reshape(b, t, MEM_HEADS, MEM_HEAD_DIM), qnorm)
    s = jnp.einsum('bthd,bmhd->bhtm', q.astype(jnp.float32), mk.astype(jnp.float32)) * MEM_SCALE
    a = jax.nn.softmax(s, axis=-1)
    o = jnp.einsum('bhtm,bmhd->bthd', a, mv.astype(jnp.float32)).astype(h.dtype)
    return o.reshape(b, t, MEM_HEADS * MEM_HEAD_DIM) @ wo


def _sq_relu_mlp(h, w1, w2):
    return jnp.square(jax.nn.relu(h @ w1)) @ w2


def _stack(xs):
    return jnp.stack(xs, axis=0) if xs else None


def _trunk(x, pos0, mem, mem_cache, bufs_a, bufs_b, sb_past, ret_states, p):
    b, t, _ = x.shape
    new_a, new_b, new_k, new_v, new_r, new_mk, new_mv = [], [], [], [], [], [], []
    for i in range(DEPTH):
        j, kind = i // N_MIXERS, i % N_MIXERS
        h = _rmsnorm(x, p['norm_mix'][i])
        if kind == 0:
            out, buf = _conformer_conv(h, bufs_a[j], p['cva_w1'][j], p['cva_b1'][j], p['cva_wdw'][j], p['cva_bdw'][j],
                                       p['cva_ln_g'][j], p['cva_ln_b'][j], p['cva_w2'][j])
            new_a.append(buf)
        elif kind == 1:
            out, buf = _short_conv(h, bufs_b[j], p['scb_win'][j], p['scb_wconv'][j], p['scb_wout'][j])
            new_b.append(buf)
        elif kind == 2:
            q, k, v = (u.reshape(b, t, SB_HEADS, SB_HEAD_DIM) for u in jnp.split(h @ p['sb_wqkv'][j], 3, axis=-1))
            if sb_past is None:
                o = _sb_prompt(q, k, v, p['sb_bias'][j])
            else:
                cache_k, cache_v, page_table = sb_past
                past_k = cache_k[j][page_table].reshape(b, -1, SB_HEADS, SB_HEAD_DIM)
                past_v = cache_v[j][page_table].reshape(b, -1, SB_HEADS, SB_HEAD_DIM)
                o = _sb_sample(q, k, v, past_k, past_v, pos0, p['sb_bias'][j])
            out = o.reshape(b, t, D_MODEL) @ p['sb_wo'][j]
            new_k.append(k)
            new_v.append(v)
        else:
            q, k, v, g = jnp.split(h @ p['ret_wqkvg'][j], [D_MODEL, 2 * D_MODEL, 4 * D_MODEL], axis=-1)
            q = _rotary(q.reshape(b, t, RET_HEADS, RET_DK), pos0)
            k = _rotary(k.reshape(b, t, RET_HEADS, RET_DK), pos0) * (RET_DK ** -0.5)
            o, s_new = _retention(q, k, v.reshape(b, t, RET_HEADS, RET_DV), ret_states[j])
            o = _layernorm(o, p['ret_gn_g'][j]).reshape(b, t, 2 * D_MODEL)
            out = (jax.nn.silu(g.astype(jnp.float32)) * o).astype(x.dtype) @ p['ret_wo'][j]
            new_r.append(s_new.astype(x.dtype))
        x = x + out
        if mem_cache is None:
            mk, mv = _mem_kv(mem, p['norm_mem'][i], p['xa_wk'][i], p['xa_wv'][i], p['xa_knorm'][i])
            new_mk.append(mk)
            new_mv.append(mv)
        else:
            mk, mv = mem_cache[0][i], mem_cache[1][i]
        x = x + _mem_cross(_rmsnorm(x, p['norm_cross'][i]), mk, mv, p['xa_wq'][i], p['xa_wo'][i], p['xa_qnorm'][i])
        x = x + _sq_relu_mlp(_rmsnorm(x, p['norm_mlp'][i]), p['mlp_w1'][i], p['mlp_w2'][i])
    return (x, _stack(new_a), _stack(new_b), _stack(new_k), _stack(new_v), _stack(new_r),
            _stack(new_mk), _stack(new_mv))


def setup_inputs(seed: int = 0) -> dict:
    key = jax.random.key(seed)
    keys = iter(jax.random.split(key, 48))

    def nrm(shape, scale=1.0):
        return jax.random.normal(next(keys), shape, jnp.float32) * scale

    def gain(shape):
        return 1.0 + nrm(shape, 0.02)

    d = D_MODEL
    dm = MEM_HEADS * MEM_HEAD_DIM
    n_pages = PAST_LEN // PAGE_SIZE
    n_used = DEC_BATCH * n_pages
    n_pool = (5 * n_used + 3) // 4
    x_prompt = nrm((BATCH, SEQ, d))
    x_sample = nrm((DEC_BATCH, DEC_SEQ, d))
    mem_prompt = nrm((BATCH, MEM_TOKENS, d))
    state_conv_a = nrm((N_CONV_A, DEC_BATCH, CONV_A_WIDTH - 1, d), 0.5)
    state_conv_b = nrm((N_CONV_B, DEC_BATCH, CONV_B_WIDTH - 1, d))
    cache_sb_k = nrm((N_SB, n_pool, PAGE_SIZE, SB_HEADS, SB_HEAD_DIM))
    cache_sb_v = nrm((N_SB, n_pool, PAGE_SIZE, SB_HEADS, SB_HEAD_DIM))
    page_table = jax.random.permutation(next(keys), n_pool)[:n_used].reshape(DEC_BATCH, n_pages).astype(jnp.int32)
    state_ret = nrm((N_RET, DEC_BATCH, RET_HEADS, RET_DK, RET_DV), 0.5)
    cache_mem_k = nrm((DEPTH, DEC_BATCH, MEM_TOKENS, MEM_HEADS, MEM_HEAD_DIM))
    cache_mem_v = nrm((DEPTH, DEC_BATCH, MEM_TOKENS, MEM_HEADS, MEM_HEAD_DIM))
    sb_bias = -jnp.linspace(3.0, 10.0, SB_HEADS, dtype=jnp.float32)[None, :] + nrm((N_SB, SB_HEADS), 0.1)
    return {
        'x_prompt': x_prompt, 'x_sample': x_sample, 'mem_prompt': mem_prompt,
        'state_conv_a': state_conv_a, 'state_conv_b': state_conv_b,
        'cache_sb_k': cache_sb_k, 'cache_sb_v': cache_sb_v, 'page_table': page_table,
        'state_ret': state_ret, 'cache_mem_k': cache_mem_k, 'cache_mem_v': cache_mem_v,
        'norm_mix': gain((DEPTH, d)), 'norm_cross': gain((DEPTH, d)),
        'norm_mem': gain((DEPTH, d)), 'norm_mlp': gain((DEPTH, d)),
        'xa_wq': nrm((DEPTH, d, dm), d ** -0.5), 'xa_wk': nrm((DEPTH, d, dm), d ** -0.5),
        'xa_wv': nrm((DEPTH, d, dm), d ** -0.5), 'xa_wo': nrm((DEPTH, dm, d), dm ** -0.5),
        'xa_qnorm': gain((DEPTH, MEM_HEAD_DIM)), 'xa_knorm': gain((DEPTH, MEM_HEAD_DIM)),
        'mlp_w1': nrm((DEPTH, d, D_FF), d ** -0.5), 'mlp_w2': nrm((DEPTH, D_FF, d), D_FF ** -0.5),
        'cva_w1': nrm((N_CONV_A, d, 2 * d), d ** -0.5), 'cva_b1': nrm((N_CONV_A, 2 * d), 0.02),
        'cva_wdw': nrm((N_CONV_A, CONV_A_WIDTH, d), CONV_A_WIDTH ** -0.5), 'cva_bdw': nrm((N_CONV_A, d), 0.02),
        'cva_ln_g': gain((N_CONV_A, d)), 'cva_ln_b': nrm((N_CONV_A, d), 0.02),
        'cva_w2': nrm((N_CONV_A, d, d), d ** -0.5),
        'scb_win': nrm((N_CONV_B, d, 3 * d), d ** -0.5),
        'scb_wconv': nrm((N_CONV_B, CONV_B_WIDTH, d), CONV_B_WIDTH ** -0.5),
        'scb_wout': nrm((N_CONV_B, d, d), d ** -0.5),
        'sb_wqkv': nrm((N_SB, d, 3 * d), d ** -0.5), 'sb_wo': nrm((N_SB, d, d), d ** -0.5),
        'sb_bias': sb_bias,
        'ret_wqkvg': nrm((N_RET, d, 6 * d), d ** -0.5), 'ret_gn_g': gain((N_RET, RET_HEADS, RET_DV)),
        'ret_wo': nrm((N_RET, 2 * d, d), (2 * d) ** -0.5),
    }


def reference(x_prompt, x_sample, mem_prompt, state_conv_a, state_conv_b, cache_sb_k, cache_sb_v, page_table,
              state_ret, cache_mem_k, cache_mem_v, norm_mix, norm_cross, norm_mem, norm_mlp, xa_wq, xa_wk, xa_wv,
              xa_wo, xa_qnorm, xa_knorm, mlp_w1, mlp_w2, cva_w1, cva_b1, cva_wdw, cva_bdw, cva_ln_g, cva_ln_b,
              cva_w2, scb_win, scb_wconv, scb_wout, sb_wqkv, sb_wo, sb_bias, ret_wqkvg, ret_gn_g, ret_wo):
    p = {
        'norm_mix': norm_mix, 'norm_cross': norm_cross, 'norm_mem': norm_mem, 'norm_mlp': norm_mlp,
        'xa_wq': xa_wq, 'xa_wk': xa_wk, 'xa_wv': xa_wv, 'xa_wo': xa_wo, 'xa_qnorm': xa_qnorm, 'xa_knorm': xa_knorm,
        'mlp_w1': mlp_w1, 'mlp_w2': mlp_w2,
        'cva_w1': cva_w1, 'cva_b1': cva_b1, 'cva_wdw': cva_wdw, 'cva_bdw': cva_bdw,
        'cva_ln_g': cva_ln_g, 'cva_ln_b': cva_ln_b, 'cva_w2': cva_w2,
        'scb_win': scb_win, 'scb_wconv': scb_wconv, 'scb_wout': scb_wout,
        'sb_wqkv': sb_wqkv, 'sb_wo': sb_wo, 'sb_bias': sb_bias,
        'ret_wqkvg': ret_wqkvg, 'ret_gn_g': ret_gn_g, 'ret_wo': ret_wo,
    }
    bp = x_prompt.shape[0]
    zeros_a = jnp.zeros((N_CONV_A, bp, CONV_A_WIDTH - 1, D_MODEL), x_prompt.dtype)
    zeros_b = jnp.zeros((N_CONV_B, bp, CONV_B_WIDTH - 1, D_MODEL), x_prompt.dtype)
    zeros_ret = jnp.zeros((N_RET, bp, RET_HEADS, RET_DK, RET_DV), jnp.float32)
    (y_prompt, conv_a_p, conv_b_p, sb_k_p, sb_v_p, ret_p, mem_k_p, mem_v_p) = _trunk(
        x_prompt, 0, mem_prompt, None, zeros_a, zeros_b, None, zeros_ret, p)
    (y_sample, conv_a_s, conv_b_s, sb_k_s, sb_v_s, ret_s, _, _) = _trunk(
        x_sample, PAST_LEN, None, (cache_mem_k, cache_mem_v), state_conv_a, state_conv_b,
        (cache_sb_k, cache_sb_v, page_table), state_ret, p)
    return (y_prompt, y_sample, conv_a_p, conv_a_s, conv_b_p, conv_b_s, sb_k_p, sb_v_p, sb_k_s, sb_v_s,
            ret_p, ret_s, mem_k_p, mem_v_p)
```

```python
import functools
import math

import jax
import jax.numpy as jnp
from jax import lax
from jax.experimental import pallas as pl
from jax.experimental.pallas import tpu as pltpu

F32, BF16 = jnp.float32, jnp.bfloat16

D_MODEL = 1024
NORM_EPS = 1e-6
PAGE_SIZE = 128
CONV_A_WIDTH, CONV_B_WIDTH = 31, 3
SB_HEADS, SB_HEAD_DIM = 16, 64
SB_SCALE = SB_HEAD_DIM ** -0.5
RET_HEADS, RET_DK, RET_DV = 4, 256, 512
ROPE_BASE = 10000.0
MEM_HEADS, MEM_HEAD_DIM = 4, 256
MEM_SCALE = MEM_HEAD_DIM ** -0.5
D_FF = 4 * D_MODEL

CONV_HALO = 32
SAMPLE_ROWS = 8
VMEM_LIMIT = 56 * 1024 * 1024
ROW_TILE = 512
CONV_TILE = 256
SB_BLOCK = 256
RET_CHUNK = 256

_NT = (((1,), (1,)), ((), ()))


def _dot(a, b):
    return jnp.dot(a.astype(BF16), b, preferred_element_type=F32)


def _dot_nt(a, b):
    return lax.dot_general(a.astype(BF16), b, _NT, preferred_element_type=F32)


def _rms(x, g):
    return x * lax.rsqrt(jnp.mean(x * x, axis=-1, keepdims=True) + NORM_EPS) * g


def _layernorm(x, g, b=None):
    xc = x - jnp.mean(x, axis=-1, keepdims=True)
    y = xc * lax.rsqrt(jnp.mean(xc * xc, axis=-1, keepdims=True) + NORM_EPS) * g
    return y if b is None else y + b


def _silu(x):
    return x * jax.nn.sigmoid(x)


def _pick_tile(n, pref):
    t = min(n, pref)
    while n % t:
        t -= 8
    return t


def _row_call(name, body, m, tm, rows, outs, consts=(), bconsts=(), prows=(), tiles_per_batch=1):
    in_specs, args = [], []
    for r in rows:
        arr, width, cb = r if isinstance(r, tuple) else (r, r.shape[1], 0)
        in_specs.append(pl.BlockSpec((tm, width), lambda i, cb=cb: (i, cb)))
        args.append(arr)
    for p in prows:
        in_specs.append(pl.BlockSpec((tm, p.shape[1]), lambda i: (i % tiles_per_batch, 0)))
        args.append(p)
    for c in bconsts:
        in_specs.append(pl.BlockSpec((None,) + c.shape[1:],
                                     lambda i, nd=c.ndim: (i // tiles_per_batch,) + (0,) * (nd - 1)))
        args.append(c)
    for c in consts:
        in_specs.append(pl.BlockSpec(c.shape, lambda i, nd=c.ndim: (0,) * nd, pipeline_mode=pl.Buffered(1)))
        args.append(c)
    n_r, n_p, n_b, n_c = len(rows), len(prows), len(bconsts), len(consts)

    def kern(*refs):
        a, b, c, d = n_r, n_r + n_p, n_r + n_p + n_b, n_r + n_p + n_b + n_c
        body(refs[:a], refs[a:b], refs[b:c], refs[c:d], refs[d:])

    return pl.pallas_call(
        kern, grid=(m // tm,), in_specs=in_specs,
        out_specs=[pl.BlockSpec((tm, w), lambda i: (i, 0)) for w, _ in outs],
        out_shape=[jax.ShapeDtypeStruct((m, w), dt) for w, dt in outs],
        compiler_params=pltpu.CompilerParams(dimension_semantics=("parallel",), vmem_limit_bytes=VMEM_LIMIT),
        name=name)(*args)


def _mem_q(x1, g_cross, wq_ref, qn):
    q = _dot(_rms(x1, g_cross), wq_ref[...])
    heads = []
    for hd in range(MEM_HEADS):
        qh = q[:, hd * MEM_HEAD_DIM:(hd + 1) * MEM_HEAD_DIM]
        heads.append(qh * lax.rsqrt(jnp.mean(qh * qh, axis=-1, keepdims=True) + NORM_EPS) * qn)
    return heads


def _mem_attend(qh, mk_h, mv_h):
    s = _dot_nt(qh, mk_h) * MEM_SCALE
    p = jnp.exp(s - jnp.max(s, axis=-1, keepdims=True))
    a = p / jnp.sum(p, axis=-1, keepdims=True)
    return _dot(a, mv_h)


def _mlp(x2, g_mlp, w1_ref, w2_ref):
    h = _rms(x2, g_mlp).astype(BF16)
    acc = x2
    for c in range(D_FF // D_MODEL):
        sl = slice(c * D_MODEL, (c + 1) * D_MODEL)
        u = jnp.dot(h, w1_ref[:, sl], preferred_element_type=F32)
        u = jnp.square(jnp.maximum(u, 0.0))
        acc = acc + _dot(u, w2_ref[sl, :])
    return acc


def _rotary(x, cos, sin):
    half = RET_DK // 2
    parts = []
    for hd in range(RET_HEADS):
        x1 = x[:, hd * RET_DK: hd * RET_DK + half]
        x2 = x[:, hd * RET_DK + half: (hd + 1) * RET_DK]
        parts += [x1 * cos - x2 * sin, x1 * sin + x2 * cos]
    return jnp.concatenate(parts, axis=-1)


def _pre_conv_a(rows, prows, bcs, cs, outs):
    g_mix, w1, b1 = cs
    h = _rms(rows[0][...], g_mix[...])
    a = _dot(h, w1[:, :D_MODEL]) + b1[:, :D_MODEL]
    gate = _dot(h, w1[:, D_MODEL:]) + b1[:, D_MODEL:]
    outs[0][...] = a * jax.nn.sigmoid(gate)


def _pre_conv_b(rows, prows, bcs, cs, outs):
    g_mix, win = cs
    h = _rms(rows[0][...], g_mix[...])
    outs[0][...] = _dot(h, win[:, :D_MODEL])
    outs[1][...] = _dot(h, win[:, D_MODEL:2 * D_MODEL]) * _dot(h, win[:, 2 * D_MODEL:])


def _pre_sb(rows, prows, bcs, cs, outs):
    g_mix, wqkv = cs
    h = _rms(rows[0][...], g_mix[...])
    q = _dot(h, wqkv[:, :D_MODEL])
    k = _dot(h, wqkv[:, D_MODEL:2 * D_MODEL])
    v = _dot(h, wqkv[:, 2 * D_MODEL:])
    outs[0][...] = q
    outs[1][...] = k
    outs[2][...] = v
    if len(outs) > 3:
        outs[3][...] = (q * SB_SCALE).astype(BF16)
        outs[4][...] = k.astype(BF16)
        outs[5][...] = v.astype(BF16)


def _pre_ret(rows, prows, bcs, cs, outs):
    g_mix, w = cs
    cos, sin = prows[0][...], prows[1][...]
    h = _rms(rows[0][...], g_mix[...])
    d = D_MODEL
    outs[0][...] = _rotary(_dot(h, w[:, :d]), cos, sin)
    outs[1][...] = _rotary(_dot(h, w[:, d:2 * d]), cos, sin) * (RET_DK ** -0.5)
    outs[2][...] = _dot(h, w[:, 2 * d:4 * d])
    outs[3][...] = _dot(h, w[:, 4 * d:])


def _post_cross_prompt(rows, prows, bcs, cs, outs):
    x_ref, mix_ref = rows
    mk_ref, mv_ref = bcs
    wo, g_cross, wq, qn, xwo = cs
    x1 = x_ref[...] + _dot(mix_ref[...], wo[...])
    acc = x1
    for hd, qh in enumerate(_mem_q(x1, g_cross[...], wq, qn[...])):
        sl = slice(hd * MEM_HEAD_DIM, (hd + 1) * MEM_HEAD_DIM)
        acc = acc + _dot(_mem_attend(qh, mk_ref[:, sl], mv_ref[:, sl]), xwo[sl, :])
    outs[0][...] = acc


def _post_q_sample(rows, prows, bcs, cs, outs):
    x_ref, mix_ref = rows
    wo, g_cross, wq, qn = cs
    x1 = x_ref[...] + _dot(mix_ref[...], wo[...])
    outs[0][...] = x1
    outs[1][...] = jnp.concatenate(_mem_q(x1, g_cross[...], wq, qn[...]), axis=-1)


def _post_mlp(rows, prows, bcs, cs, outs):
    g_mlp, w1, w2 = cs
    outs[0][...] = _mlp(rows[0][...], g_mlp[...], w1, w2)


def _post_xo_mlp(rows, prows, bcs, cs, outs):
    xwo, g_mlp, w1, w2 = cs
    x2 = rows[0][...] + _dot(rows[1][...], xwo[...])
    outs[0][...] = _mlp(x2, g_mlp[...], w1, w2)


def _mem_kv_call(mem, norm_mem, wk, wv, knorm):
    depth = wk.shape[0]
    m = mem.shape[0]

    def kern(mem_ref, g_ref, wk_ref, wv_ref, kn_ref, k_ref, v_ref, kb_ref, vb_ref):
        mn = _rms(mem_ref[...], g_ref[...])
        k = _dot(mn, wk_ref[...])
        kn = kn_ref[...]
        heads = []
        for hd in range(MEM_HEADS):
            kh = k[:, hd * MEM_HEAD_DIM:(hd + 1) * MEM_HEAD_DIM]
            heads.append(kh * lax.rsqrt(jnp.mean(kh * kh, axis=-1, keepdims=True) + NORM_EPS) * kn)
        k = jnp.concatenate(heads, axis=-1)
        v = _dot(mn, wv_ref[...])
        k_ref[...] = k
        v_ref[...] = v
        kb_ref[...] = k.astype(BF16)
        vb_ref[...] = v.astype(BF16)

    lay = lambda *shape: pl.BlockSpec((None,) + shape, lambda i: (i,) + (0,) * len(shape))
    return pl.pallas_call(
        kern, grid=(depth,),
        in_specs=[pl.BlockSpec(mem.shape, lambda i: (0, 0)), lay(1, D_MODEL), lay(D_MODEL, D_MODEL),
                  lay(D_MODEL, D_MODEL), lay(1, MEM_HEAD_DIM)],
        out_specs=[lay(m, D_MODEL)] * 4,
        out_shape=[jax.ShapeDtypeStruct((depth, m, D_MODEL), F32)] * 2
        + [jax.ShapeDtypeStruct((depth, m, D_MODEL), BF16)] * 2,
        compiler_params=pltpu.CompilerParams(dimension_semantics=("parallel",), vmem_limit_bytes=VMEM_LIMIT),
        name="mem_kv")(mem, norm_mem, wk, wv, knorm)


def _conv_call(name, xin, state, w, epilogue, tm, valid_last, extra_rows=(), consts=()):
    b, t, d = xin.shape
    ktaps = w.shape[0]
    off = CONV_HALO - (ktaps - 1)
    nt = t // tm
    rb = min(16, tm)
    state_pad = jnp.pad(state, ((0, 0), (off, 0), (0, 0)))
    n_e, n_c = len(extra_rows), len(consts)
    shifts = sorted({(off + k) % 8 for k in range(ktaps)} - {0})
    span = tm + CONV_HALO - 8

    def kern(x_ref, st_ref, w_ref, *rest):
        extra, cs = rest[:n_e], rest[n_e:n_e + n_c]
        o_ref, ns_ref, xpad, xs = rest[n_e + n_c:]
        ti = pl.program_id(1)

        @pl.when(ti == 0)
        def _():
            xpad[0:CONV_HALO, :] = st_ref[...]

        xpad[CONV_HALO:CONV_HALO + tm, :] = x_ref[...]
        for si, r in enumerate(shifts):
            xs[si, 0:span, :] = xpad[r:r + span, :]

        def chunk(c, carry):
            r0 = pl.multiple_of(c * rb, rb)
            acc = jnp.zeros((rb, d), F32)
            for k in range(ktaps):
                base, r = (off + k) // 8 * 8, (off + k) % 8
                rows = pl.ds(pl.multiple_of(r0 + base, 8), rb)
                src = xpad[rows, :] if r == 0 else xs[shifts.index(r), rows, :]
                acc = acc + w_ref[k:k + 1, :] * src
            o_ref[pl.ds(r0, rb), :] = epilogue(acc, [e[pl.ds(r0, rb), :] for e in extra], cs)
            return carry

        lax.fori_loop(0, tm // rb, chunk, 0)

        @pl.when(ti == nt - 1)
        def _():
            ns_ref[...] = xpad[valid_last:valid_last + CONV_HALO, :]

        @pl.when(ti < nt - 1)
        def _():
            xpad[0:CONV_HALO, :] = xpad[tm:tm + CONV_HALO, :]

    tile = pl.BlockSpec((None, tm, d), lambda bi, ti: (bi, ti, 0))
    per_seq = pl.BlockSpec((None, CONV_HALO, d), lambda bi, ti: (bi, 0, 0))
    const = lambda c: pl.BlockSpec(c.shape, lambda bi, ti, nd=c.ndim: (0,) * nd)
    out, ns = pl.pallas_call(
        kern, grid=(b, nt),
        in_specs=[tile, per_seq, const(w)] + [tile] * n_e + [const(c) for c in consts],
        out_specs=[tile, per_seq],
        out_shape=[jax.ShapeDtypeStruct((b, t, d), F32), jax.ShapeDtypeStruct((b, CONV_HALO, d), F32)],
        scratch_shapes=[pltpu.VMEM((CONV_HALO + tm, d), F32), pltpu.VMEM((len(shifts), CONV_HALO + tm, d), F32)],
        compiler_params=pltpu.CompilerParams(dimension_semantics=("parallel", "arbitrary"),
                                             vmem_limit_bytes=VMEM_LIMIT),
        name=name)(xin, state_pad, w, *extra_rows, *consts)
    return out, ns[:, off:, :]


def _conv_a_epilogue(acc, extra, cs):
    bdw, ln_g, ln_b = cs
    return _silu(_layernorm(acc + bdw[...], ln_g[...], ln_b[...]))


def _conv_b_epilogue(acc, extra, cs):
    return extra[0] * acc


def _log_keep(z):
    return -(jnp.maximum(z, 0.0) + jnp.log(1.0 + jnp.exp(-jnp.abs(z))))


def _suffix_sum(lk, u):
    hi = lk.astype(BF16)
    lo = (lk - hi.astype(F32)).astype(BF16)
    return jnp.dot(hi, u, preferred_element_type=F32) + jnp.dot(lo, u, preferred_element_type=F32)


def _strict_lower(n):
    return (jnp.arange(n)[:, None] > jnp.arange(n)[None, :]).astype(BF16)


def _sb_prompt_call(qb, kb, vb, bias, blk):
    b, t, d = qb.shape
    nq = t // blk
    pairs = SB_HEADS // 2
    hw = 2 * SB_HEAD_DIM
    u = _strict_lower(blk)

    def kern(bias_ref, q_ref, k_ref, v_ref, u_ref, o_ref):
        hp, qi = pl.program_id(1), pl.program_id(2)
        tri = lax.broadcasted_iota(jnp.int32, (blk, blk), 1) < lax.broadcasted_iota(jnp.int32, (blk, blk), 0)
        uu = u_ref[...]
        outs = []
        for hh in range(2):
            sl = slice(hh * SB_HEAD_DIM, (hh + 1) * SB_HEAD_DIM)
            qh = q_ref[:, sl]
            bias_h = bias_ref[2 * hp + hh]

            def block(kj, carry, acc, mask):
                r0 = pl.multiple_of(kj * blk, blk)
                z = lax.dot_general(qh, k_ref[pl.ds(r0, blk), sl], _NT, preferred_element_type=F32) + bias_h
                lk = _log_keep(z)
                if mask is not None:
                    lk = jnp.where(mask, lk, 0.0)
                a = jnp.exp(z + lk + _suffix_sum(lk, uu) + carry)
                if mask is not None:
                    a = jnp.where(mask, a, 0.0)
                acc = acc + jnp.dot(a.astype(BF16), v_ref[pl.ds(r0, blk), sl], preferred_element_type=F32)
                return carry + jnp.sum(lk, axis=-1, keepdims=True), acc

            carry, acc = block(qi, jnp.zeros((blk, 1), F32), jnp.zeros((blk, SB_HEAD_DIM), F32), tri)
            carry, acc = lax.fori_loop(0, qi, lambda it, c: block(qi - 1 - it, c[0], c[1], None), (carry, acc))
            outs.append(acc)
        o_ref[...] = jnp.concatenate(outs, axis=-1)

    tile = pl.BlockSpec((None, blk, hw), lambda bi, hp, qi: (bi, qi, hp))
    seq = pl.BlockSpec((None, t, hw), lambda bi, hp, qi: (bi, 0, hp))
    return pl.pallas_call(
        kern, grid=(b, pairs, nq),
        in_specs=[pl.BlockSpec(memory_space=pltpu.SMEM), tile, seq, seq,
                  pl.BlockSpec((blk, blk), lambda bi, hp, qi: (0, 0))],
        out_specs=tile,
        out_shape=jax.ShapeDtypeStruct((b, t, d), F32),
        compiler_params=pltpu.CompilerParams(dimension_semantics=("parallel", "parallel", "arbitrary"),
                                             vmem_limit_bytes=VMEM_LIMIT),
        name="sb_prompt")(bias, qb, kb, vb, u)


def _sb_sample_call(q, k_new, v_new, cache_k, cache_v, page_table, bias, n_new):
    bsz, rows, d = q.shape
    n_pages = page_table.shape[1]
    nr = rows * SB_HEADS
    q_rep = jnp.repeat(q, SB_HEADS, axis=1)
    bias_col = jnp.tile(bias, rows)[:, None]
    u = _strict_lower(PAGE_SIZE)

    def kern(pt_ref, q_ref, kn_ref, vn_ref, kp_ref, vp_ref, bias_ref, u_ref, o_ref, qbd, acc, carry):
        p = pl.program_id(1)
        row = lax.broadcasted_iota(jnp.int32, (nr, d), 0)
        lane = lax.broadcasted_iota(jnp.int32, (nr, d), 1)
        diag = (row % SB_HEADS) == (lane // SB_HEAD_DIM)

        def page(kp, vp, mask):
            z = lax.dot_general(qbd[...], kp, _NT, preferred_element_type=F32) + bias_ref[...]
            lk = _log_keep(z)
            if mask is not None:
                lk = jnp.where(mask, lk, 0.0)
            a = jnp.exp(z + lk + _suffix_sum(lk, u_ref[...]) + carry[...])
            if mask is not None:
                a = jnp.where(mask, a, 0.0)
            acc[...] += jnp.dot(a.astype(BF16), vp, preferred_element_type=F32)
            carry[...] += jnp.sum(lk, axis=-1, keepdims=True)

        @pl.when(p == 0)
        def _():
            qbd[...] = jnp.where(diag, q_ref[...] * SB_SCALE, 0.0).astype(BF16)
            acc[...] = jnp.zeros_like(acc)
            carry[...] = jnp.zeros_like(carry)
            pad = jnp.zeros((PAGE_SIZE - rows, d), F32)
            kn = jnp.concatenate([kn_ref[...], pad], axis=0).astype(BF16)
            vn = jnp.concatenate([vn_ref[...], pad], axis=0).astype(BF16)
            key = lax.broadcasted_iota(jnp.int32, (nr, PAGE_SIZE), 1)
            tq = lax.broadcasted_iota(jnp.int32, (nr, PAGE_SIZE), 0) // SB_HEADS
            page(kn, vn, (key < tq) & (key < n_new))

        page(kp_ref[...].astype(BF16), vp_ref[...].astype(BF16), None)

        @pl.when(p == n_pages - 1)
        def _():
            o_ref[...] = jnp.sum(jnp.where(diag, acc[...], 0.0).reshape(rows, SB_HEADS, d), axis=1)

    seq = lambda r: pl.BlockSpec((None, r, d), lambda bi, p, pt: (bi, 0, 0))
    pagespec = pl.BlockSpec((None, PAGE_SIZE, d), lambda bi, p, pt: (pt[bi, n_pages - 1 - p], 0, 0))
    const = lambda c: pl.BlockSpec(c.shape, lambda bi, p, pt: (0, 0))
    return pl.pallas_call(
        kern,
        grid_spec=pltpu.PrefetchScalarGridSpec(
            num_scalar_prefetch=1, grid=(bsz, n_pages),
            in_specs=[seq(nr), seq(rows), seq(rows), pagespec, pagespec, const(bias_col), const(u)],
            out_specs=seq(rows),
            scratch_shapes=[pltpu.VMEM((nr, d), BF16), pltpu.VMEM((nr, d), F32), pltpu.VMEM((nr, 1), F32)]),
        out_shape=jax.ShapeDtypeStruct((bsz, rows, d), F32),
        compiler_params=pltpu.CompilerParams(dimension_semantics=("parallel", "arbitrary"),
                                             vmem_limit_bytes=VMEM_LIMIT),
        name="sb_sample")(page_table, q_rep, k_new, v_new, cache_k, cache_v, bias_col, u)


def _ret_tables(c, valid):
    log_gamma = jnp.log(1.0 - jnp.power(2.0, -5.0 - jnp.arange(RET_HEADS, dtype=F32)))
    idx = jnp.arange(c, dtype=F32)
    diff = idx[:, None] - idx[None, :]
    intra = jnp.where(diff >= 0, jnp.exp(jnp.maximum(diff, 0.0)[None] * log_gamma[:, None, None]), 0.0)
    q_decay = jnp.exp((idx + 1.0)[None, :] * log_gamma[:, None])[..., None]
    k_decay = jnp.where(idx < valid, jnp.exp((valid - 1.0 - idx)[None, :] * log_gamma[:, None]), 0.0)[..., None]
    chunk_decay = jnp.exp(valid * log_gamma)
    return intra, q_decay, k_decay, chunk_decay


def _retention_call(q, k, v, g, s0, gn_g, c, valid):
    b, t, _ = q.shape
    nc = t // c
    intra, q_decay, k_decay, chunk_decay = _ret_tables(c, valid)

    def kern(cd_ref, q_ref, k_ref, v_ref, g_ref, s0_ref, gn_ref, in_ref, qd_ref, kd_ref, y_ref, s_ref):
        h, ci = pl.program_id(1), pl.program_id(2)

        @pl.when(ci == 0)
        def _():
            s_ref[...] = s0_ref[...]

        state = s_ref[...]
        qc, kc = q_ref[...].astype(BF16), k_ref[...]
        vc = v_ref[...].astype(BF16)
        scores = lax.dot_general(qc, kc.astype(BF16), _NT, preferred_element_type=F32) * in_ref[...]
        o = (_dot(scores, vc) + _dot(qc, state.astype(BF16)) * qd_ref[...])
        s_ref[...] = state * cd_ref[h] + _dot((kc * kd_ref[...]).T, vc)
        y_ref[...] = _silu(g_ref[...]) * _layernorm(o, gn_ref[...])

    qk = pl.BlockSpec((None, c, RET_DK), lambda bi, h, ci: (bi, ci, h))
    vg = pl.BlockSpec((None, c, RET_DV), lambda bi, h, ci: (bi, ci, h))
    st = pl.BlockSpec((None, None, RET_DK, RET_DV), lambda bi, h, ci: (bi, h, 0, 0))
    per_head = lambda *shape: pl.BlockSpec((None,) + shape, lambda bi, h, ci: (h,) + (0,) * len(shape))
    return pl.pallas_call(
        kern, grid=(b, RET_HEADS, nc),
        in_specs=[pl.BlockSpec(memory_space=pltpu.SMEM), qk, qk, vg, vg, st, per_head(1, RET_DV),
                  per_head(c, c), per_head(c, 1), per_head(c, 1)],
        out_specs=[vg, st],
        out_shape=[jax.ShapeDtypeStruct((b, t, RET_HEADS * RET_DV), F32),
                   jax.ShapeDtypeStruct((b, RET_HEADS, RET_DK, RET_DV), F32)],
        compiler_params=pltpu.CompilerParams(dimension_semantics=("parallel", "parallel", "arbitrary"),
                                             vmem_limit_bytes=VMEM_LIMIT),
        name="retention")(chunk_decay, q, k, v, g, s0, gn_g, intra, q_decay, k_decay)


def _mem_attend_sample_call(q, cache_k, cache_v, layer):
    bsz, rows, d = q.shape
    m = cache_k.shape[2]

    def kern(q_ref, k_ref, v_ref, o_ref):
        qv = q_ref[...]
        heads = []
        for hd in range(MEM_HEADS):
            sl = slice(hd * MEM_HEAD_DIM, (hd + 1) * MEM_HEAD_DIM)
            heads.append(_mem_attend(qv[:, sl], k_ref[:, sl].astype(BF16), v_ref[:, sl].astype(BF16)))
        o_ref[...] = jnp.concatenate(heads, axis=-1)

    seq = pl.BlockSpec((None, rows, d), lambda bi: (bi, 0, 0))
    kv = pl.BlockSpec((None, None, m, d), lambda bi: (layer, bi, 0, 0))
    return pl.pallas_call(
        kern, grid=(bsz,), in_specs=[seq, kv, kv], out_specs=seq,
        out_shape=jax.ShapeDtypeStruct((bsz, rows, d), F32),
        compiler_params=pltpu.CompilerParams(dimension_semantics=("parallel",), vmem_limit_bytes=VMEM_LIMIT),
        name="mem_attend_sample")(q, cache_k, cache_v)


def _rope_tables(pos):
    half = RET_DK // 2
    inv_freq = jnp.power(ROPE_BASE, -jnp.arange(0, RET_DK, 2, dtype=F32) / RET_DK)
    ang = pos.astype(F32)[:, None] * inv_freq[None, :]
    return jnp.cos(ang), jnp.sin(ang)


def _trunk(x, t_valid, pos0, p, states, sb_past, mem):
    b, t, d = x.shape
    m = b * t
    tm = _pick_tile(t, ROW_TILE)
    tpb = t // tm
    if mem[0] == 'sample':
        tm, tpb = m, 1
    state_a, state_b, state_r = states
    row = lambda r: r.reshape(m, -1)
    vec = lambda g: g.reshape(1, -1)
    xf = row(x)
    ct = _pick_tile(t, CONV_TILE)
    new = {}
    for i in range(4):
        g_mix = vec(p['norm_mix'][i])
        if i == 0:
            (gl,) = _row_call("pre_conv_a", _pre_conv_a, m, tm, [xf], [(d, F32)],
                              consts=[g_mix, p['cva_w1'], vec(p['cva_b1'])])
            mix, new['a'] = _conv_call("conv_a", gl.reshape(b, t, d), state_a, p['cva_wdw'], _conv_a_epilogue,
                                       ct, t_valid if t_valid < t else ct,
                                       consts=[vec(p['cva_bdw']), vec(p['cva_ln_g']), vec(p['cva_ln_b'])])
            wo = p['cva_w2']
        elif i == 1:
            bg, cu = _row_call("pre_conv_b", _pre_conv_b, m, tm, [xf], [(d, F32)] * 2, consts=[g_mix, p['scb_win']])
            mix, new['b'] = _conv_call("conv_b", cu.reshape(b, t, d), state_b, p['scb_wconv'], _conv_b_epilogue,
                                       ct, t_valid if t_valid < t else ct,
                                       extra_rows=[bg.reshape(b, t, d)])
            wo = p['scb_wout']
        elif i == 2:
            if sb_past is None:
                q, k, v, qb, kb, vb = _row_call("pre_sb", _pre_sb, m, tm, [xf], [(d, F32)] * 3 + [(d, BF16)] * 3,
                                                consts=[g_mix, p['sb_wqkv']])
                mix = _sb_prompt_call(qb.reshape(b, t, d), kb.reshape(b, t, d), vb.reshape(b, t, d), p['sb_bias'],
                                      _pick_tile(t, SB_BLOCK))
            else:
                q, k, v = _row_call("pre_sb", _pre_sb, m, tm, [xf], [(d, F32)] * 3, consts=[g_mix, p['sb_wqkv']])
                cache_k, cache_v, page_table = sb_past
                mix = _sb_sample_call(q.reshape(b, t, d), k.reshape(b, t, d), v.reshape(b, t, d), cache_k, cache_v,
                                      page_table, p['sb_bias'], t_valid)
            new['k'], new['v'] = k.reshape(b, t, d), v.reshape(b, t, d)
            wo = p['sb_wo']
        else:
            cos, sin = _rope_tables(pos0 + jnp.arange(t))
            if mem[0] == 'sample':
                cos, sin = jnp.tile(cos, (b, 1)), jnp.tile(sin, (b, 1))
            q, k, v, g = _row_call("pre_ret", _pre_ret, m, tm, [xf], [(d, F32)] * 2 + [(2 * d, F32)] * 2,
                                   consts=[g_mix, p['ret_wqkvg']], prows=[cos, sin], tiles_per_batch=tpb)
            c = _pick_tile(t, RET_CHUNK)
            mix, new['r'] = _retention_call(q.reshape(b, t, d), k.reshape(b, t, d), v.reshape(b, t, 2 * d),
                                            g.reshape(b, t, 2 * d), state_r, p['ret_gn_g'].reshape(RET_HEADS, 1, RET_DV),
                                            c, min(c, t_valid))
            wo = p['ret_wo']
        mix = row(mix)
        g_cross, qn = vec(p['norm_cross'][i]), vec(p['xa_qnorm'][i])
        mlp_consts = [vec(p['norm_mlp'][i]), p['mlp_w1'][i], p['mlp_w2'][i]]
        if mem[0] == 'prompt':
            (x2,) = _row_call("post_cross", _post_cross_prompt, m, tm, [xf, mix], [(d, F32)],
                              consts=[wo, g_cross, p['xa_wq'][i], qn, p['xa_wo'][i]],
                              bconsts=[mem[1][i], mem[2][i]], tiles_per_batch=tpb)
            (xf,) = _row_call("post_mlp", _post_mlp, m, tm, [x2], [(d, F32)], consts=mlp_consts)
        else:
            x1, qm = _row_call("post_q", _post_q_sample, m, tm, [xf, mix], [(d, F32)] * 2,
                               consts=[wo, g_cross, p['xa_wq'][i], qn])
            o = _mem_attend_sample_call(qm.reshape(b, t, d), mem[1], mem[2], i)
            (xf,) = _row_call("post_xo_mlp", _post_xo_mlp, m, tm, [x1, row(o)], [(d, F32)],
                              consts=[p['xa_wo'][i]] + mlp_consts)
    return xf.reshape(b, t, d), new


def kernel(x_prompt, x_sample, mem_prompt, state_conv_a, state_conv_b, cache_sb_k, cache_sb_v, page_table, state_ret, cache_mem_k, cache_mem_v, norm_mix, norm_cross, norm_mem, norm_mlp, xa_wq, xa_wk, xa_wv, xa_wo, xa_qnorm, xa_knorm, mlp_w1, mlp_w2, cva_w1, cva_b1, cva_wdw, cva_bdw, cva_ln_g, cva_ln_b, cva_w2, scb_win, scb_wconv, scb_wout, sb_wqkv, sb_wo, sb_bias, ret_wqkvg, ret_gn_g, ret_wo):
    bp, seq, d = x_prompt.shape
    bs, dec_seq, _ = x_sample.shape
    depth = norm_mix.shape[0]
    assert depth == 4 and d == D_MODEL and dec_seq <= SAMPLE_ROWS
    n_mem = mem_prompt.shape[1]
    past_len = page_table.shape[1] * PAGE_SIZE
    wb = lambda w: w.astype(BF16)
    p = {
        'norm_mix': norm_mix, 'norm_cross': norm_cross, 'norm_mlp': norm_mlp,
        'xa_wq': wb(xa_wq), 'xa_wo': wb(xa_wo), 'xa_qnorm': xa_qnorm, 'mlp_w1': wb(mlp_w1), 'mlp_w2': wb(mlp_w2),
        'cva_w1': wb(cva_w1[0]), 'cva_b1': cva_b1[0], 'cva_wdw': cva_wdw[0], 'cva_bdw': cva_bdw[0],
        'cva_ln_g': cva_ln_g[0], 'cva_ln_b': cva_ln_b[0], 'cva_w2': wb(cva_w2[0]),
        'scb_win': wb(scb_win[0]), 'scb_wconv': scb_wconv[0], 'scb_wout': wb(scb_wout[0]),
        'sb_wqkv': wb(sb_wqkv[0]), 'sb_wo': wb(sb_wo[0]), 'sb_bias': sb_bias[0],
        'ret_wqkvg': wb(ret_wqkvg[0]), 'ret_gn_g': ret_gn_g[0], 'ret_wo': wb(ret_wo[0]),
    }

    mk, mv, mkb, mvb = _mem_kv_call(mem_prompt.reshape(bp * n_mem, d), norm_mem.reshape(depth, 1, d), wb(xa_wk),
                                    wb(xa_wv), xa_knorm.reshape(depth, 1, MEM_HEAD_DIM))
    per_seq = lambda a: a.reshape(depth, bp, n_mem, d)
    zeros = lambda *s: jnp.zeros(s, F32)
    y_p, new_p = _trunk(x_prompt, seq, 0, p,
                        (zeros(bp, CONV_A_WIDTH - 1, d), zeros(bp, CONV_B_WIDTH - 1, d),
                         zeros(bp, RET_HEADS, RET_DK, RET_DV)),
                        None, ('prompt', per_seq(mkb), per_seq(mvb)))

    xs = jnp.pad(x_sample, ((0, 0), (0, SAMPLE_ROWS - dec_seq), (0, 0)))
    pool = cache_sb_k.shape[1]
    y_s, new_s = _trunk(xs, dec_seq, past_len, p, (state_conv_a[0], state_conv_b[0], state_ret[0]),
                        (cache_sb_k[0].reshape(pool, PAGE_SIZE, d), cache_sb_v[0].reshape(pool, PAGE_SIZE, d),
                         page_table),
                        ('sample', cache_mem_k.reshape(depth, bs, n_mem, d), cache_mem_v.reshape(depth, bs, n_mem, d)))

    heads = lambda a, n: a[:, :n].reshape(1, a.shape[0], n, SB_HEADS, SB_HEAD_DIM)
    mem_out = lambda a: a.reshape(depth, bp, n_mem, MEM_HEADS, MEM_HEAD_DIM)
    return (y_p, y_s[:, :dec_seq], new_p['a'][None], new_s['a'][None], new_p['b'][None], new_s['b'][None],
            heads(new_p['k'], seq), heads(new_p['v'], seq), heads(new_s['k'], dec_seq), heads(new_s['v'], dec_seq),
            new_p['r'][None], new_s['r'][None], mem_out(mk), mem_out(mv))
```

```python
import functools
import math

import jax
import jax.numpy as jnp
from jax import lax
from jax.experimental import pallas as pl
from jax.experimental.pallas import tpu as pltpu

F32, BF16 = jnp.float32, jnp.bfloat16

D_MODEL = 1024
NORM_EPS = 1e-6
PAGE_SIZE = 128
CONV_A_WIDTH, CONV_B_WIDTH = 31, 3
SB_HEADS, SB_HEAD_DIM = 16, 64
SB_SCALE = SB_HEAD_DIM ** -0.5
RET_HEADS, RET_DK, RET_DV = 4, 256, 512
ROPE_BASE = 10000.0
MEM_HEADS, MEM_HEAD_DIM = 4, 256
MEM_SCALE = MEM_HEAD_DIM ** -0.5
D_FF = 4 * D_MODEL

CONV_HALO = 32
SAMPLE_ROWS = 8
VMEM_LIMIT = 56 * 1024 * 1024
ROW_TILE = 512
CONV_TILE = 256
SB_BLOCK = 256
SB_HEADS_PER_STEP = 4
SB_MASKED_LOGIT = -1e30
SB_PAGES_PER_STEP = 4
RET_CHUNK = 256

_NT = (((1,), (1,)), ((), ()))


def _dot(a, b):
    return jnp.dot(a.astype(BF16), b, preferred_element_type=F32)


def _dot_nt(a, b):
    return lax.dot_general(a.astype(BF16), b, _NT, preferred_element_type=F32)


def _rms(x, g):
    return x * lax.rsqrt(jnp.mean(x * x, axis=-1, keepdims=True) + NORM_EPS) * g


def _layernorm(x, g, b=None):
    xc = x - jnp.mean(x, axis=-1, keepdims=True)
    y = xc * lax.rsqrt(jnp.mean(xc * xc, axis=-1, keepdims=True) + NORM_EPS) * g
    return y if b is None else y + b


def _silu(x):
    return x * jax.nn.sigmoid(x)


def _pick_tile(n, pref):
    t = min(n, pref)
    while n % t:
        t -= 8
    return t


def _row_call(name, body, m, tm, rows, outs, consts=(), bconsts=(), prows=(), tiles_per_batch=1):
    in_specs, args = [], []
    for r in rows:
        arr, width, cb = r if isinstance(r, tuple) else (r, r.shape[1], 0)
        in_specs.append(pl.BlockSpec((tm, width), lambda i, cb=cb: (i, cb)))
        args.append(arr)
    for p in prows:
        in_specs.append(pl.BlockSpec((tm, p.shape[1]), lambda i: (i % tiles_per_batch, 0)))
        args.append(p)
    for c in bconsts:
        in_specs.append(pl.BlockSpec((None,) + c.shape[1:],
                                     lambda i, nd=c.ndim: (i // tiles_per_batch,) + (0,) * (nd - 1)))
        args.append(c)
    for c in consts:
        in_specs.append(pl.BlockSpec(c.shape, lambda i, nd=c.ndim: (0,) * nd, pipeline_mode=pl.Buffered(1)))
        args.append(c)
    n_r, n_p, n_b, n_c = len(rows), len(prows), len(bconsts), len(consts)

    def kern(*refs):
        a, b, c, d = n_r, n_r + n_p, n_r + n_p + n_b, n_r + n_p + n_b + n_c
        body(refs[:a], refs[a:b], refs[b:c], refs[c:d], refs[d:])

    return pl.pallas_call(
        kern, grid=(m // tm,), in_specs=in_specs,
        out_specs=[pl.BlockSpec((tm, w), lambda i: (i, 0)) for w, _ in outs],
        out_shape=[jax.ShapeDtypeStruct((m, w), dt) for w, dt in outs],
        compiler_params=pltpu.CompilerParams(dimension_semantics=("parallel",), vmem_limit_bytes=VMEM_LIMIT),
        name=name)(*args)


def _mem_q(x1, g_cross, wq_ref, qn):
    q = _dot(_rms(x1, g_cross), wq_ref[...])
    heads = []
    for hd in range(MEM_HEADS):
        qh = q[:, hd * MEM_HEAD_DIM:(hd + 1) * MEM_HEAD_DIM]
        heads.append(qh * lax.rsqrt(jnp.mean(qh * qh, axis=-1, keepdims=True) + NORM_EPS) * qn)
    return heads


def _mem_attend(qh, mk_h, mv_h):
    s = _dot_nt(qh, mk_h) * MEM_SCALE
    p = jnp.exp(s - jnp.max(s, axis=-1, keepdims=True))
    a = p / jnp.sum(p, axis=-1, keepdims=True)
    return _dot(a, mv_h)


def _mlp(x2, g_mlp, w1_ref, w2_ref):
    h = _rms(x2, g_mlp).astype(BF16)
    acc = x2
    for c in range(D_FF // D_MODEL):
        sl = slice(c * D_MODEL, (c + 1) * D_MODEL)
        u = jnp.dot(h, w1_ref[:, sl], preferred_element_type=F32)
        u = jnp.square(jnp.maximum(u, 0.0))
        acc = acc + _dot(u, w2_ref[sl, :])
    return acc


def _rotary(x, cos, sin):
    half = RET_DK // 2
    parts = []
    for hd in range(RET_HEADS):
        x1 = x[:, hd * RET_DK: hd * RET_DK + half]
        x2 = x[:, hd * RET_DK + half: (hd + 1) * RET_DK]
        parts += [x1 * cos - x2 * sin, x1 * sin + x2 * cos]
    return jnp.concatenate(parts, axis=-1)


def _pre_conv_a(rows, prows, bcs, cs, outs):
    g_mix, w1, b1 = cs
    h = _rms(rows[0][...], g_mix[...])
    a = _dot(h, w1[:, :D_MODEL]) + b1[:, :D_MODEL]
    gate = _dot(h, w1[:, D_MODEL:]) + b1[:, D_MODEL:]
    outs[0][...] = a * jax.nn.sigmoid(gate)


def _pre_conv_b(rows, prows, bcs, cs, outs):
    g_mix, win = cs
    h = _rms(rows[0][...], g_mix[...])
    outs[0][...] = _dot(h, win[:, :D_MODEL])
    outs[1][...] = _dot(h, win[:, D_MODEL:2 * D_MODEL]) * _dot(h, win[:, 2 * D_MODEL:])


def _pre_sb(rows, prows, bcs, cs, outs):
    g_mix, wqkv = cs
    h = _rms(rows[0][...], g_mix[...])
    q = _dot(h, wqkv[:, :D_MODEL])
    k = _dot(h, wqkv[:, D_MODEL:2 * D_MODEL])
    v = _dot(h, wqkv[:, 2 * D_MODEL:])
    outs[0][...] = q
    outs[1][...] = k
    outs[2][...] = v
    if len(outs) > 3:
        outs[3][...] = (q * SB_SCALE).astype(BF16)
        outs[4][...] = k.astype(BF16)
        outs[5][...] = v.astype(BF16)


def _pre_ret(rows, prows, bcs, cs, outs):
    g_mix, w = cs
    cos, sin = prows[0][...], prows[1][...]
    h = _rms(rows[0][...], g_mix[...])
    d = D_MODEL
    outs[0][...] = _rotary(_dot(h, w[:, :d]), cos, sin)
    outs[1][...] = _rotary(_dot(h, w[:, d:2 * d]), cos, sin) * (RET_DK ** -0.5)
    outs[2][...] = _dot(h, w[:, 2 * d:4 * d])
    outs[3][...] = _dot(h, w[:, 4 * d:])


def _post_cross_prompt(rows, prows, bcs, cs, outs):
    x_ref, mix_ref = rows
    mk_ref, mv_ref = bcs
    wo, g_cross, wq, qn, xwo = cs
    x1 = x_ref[...] + _dot(mix_ref[...], wo[...])
    acc = x1
    for hd, qh in enumerate(_mem_q(x1, g_cross[...], wq, qn[...])):
        sl = slice(hd * MEM_HEAD_DIM, (hd + 1) * MEM_HEAD_DIM)
        acc = acc + _dot(_mem_attend(qh, mk_ref[:, sl], mv_ref[:, sl]), xwo[sl, :])
    outs[0][...] = acc


def _post_q_sample(rows, prows, bcs, cs, outs):
    x_ref, mix_ref = rows
    wo, g_cross, wq, qn = cs
    x1 = x_ref[...] + _dot(mix_ref[...], wo[...])
    outs[0][...] = x1
    outs[1][...] = jnp.concatenate(_mem_q(x1, g_cross[...], wq, qn[...]), axis=-1)


def _post_mlp(rows, prows, bcs, cs, outs):
    g_mlp, w1, w2 = cs
    outs[0][...] = _mlp(rows[0][...], g_mlp[...], w1, w2)


def _post_xo_mlp(rows, prows, bcs, cs, outs):
    xwo, g_mlp, w1, w2 = cs
    x2 = rows[0][...] + _dot(rows[1][...], xwo[...])
    outs[0][...] = _mlp(x2, g_mlp[...], w1, w2)


def _mem_kv_call(mem, norm_mem, wk, wv, knorm):
    depth = wk.shape[0]
    m = mem.shape[0]

    def kern(mem_ref, g_ref, wk_ref, wv_ref, kn_ref, k_ref, v_ref, kb_ref, vb_ref):
        mn = _rms(mem_ref[...], g_ref[...])
        k = _dot(mn, wk_ref[...])
        kn = kn_ref[...]
        heads = []
        for hd in range(MEM_HEADS):
            kh = k[:, hd * MEM_HEAD_DIM:(hd + 1) * MEM_HEAD_DIM]
            heads.append(kh * lax.rsqrt(jnp.mean(kh * kh, axis=-1, keepdims=True) + NORM_EPS) * kn)
        k = jnp.concatenate(heads, axis=-1)
        v = _dot(mn, wv_ref[...])
        k_ref[...] = k
        v_ref[...] = v
        kb_ref[...] = k.astype(BF16)
        vb_ref[...] = v.astype(BF16)

    lay = lambda *shape: pl.BlockSpec((None,) + shape, lambda i: (i,) + (0,) * len(shape))
    return pl.pallas_call(
        kern, grid=(depth,),
        in_specs=[pl.BlockSpec(mem.shape, lambda i: (0, 0)), lay(1, D_MODEL), lay(D_MODEL, D_MODEL),
                  lay(D_MODEL, D_MODEL), lay(1, MEM_HEAD_DIM)],
        out_specs=[lay(m, D_MODEL)] * 4,
        out_shape=[jax.ShapeDtypeStruct((depth, m, D_MODEL), F32)] * 2
        + [jax.ShapeDtypeStruct((depth, m, D_MODEL), BF16)] * 2,
        compiler_params=pltpu.CompilerParams(dimension_semantics=("parallel",), vmem_limit_bytes=VMEM_LIMIT),
        name="mem_kv")(mem, norm_mem, wk, wv, knorm)


def _conv_call(name, xin, state, w, epilogue, tm, valid_last, extra_rows=(), consts=()):
    b, t, d = xin.shape
    ktaps = w.shape[0]
    off = CONV_HALO - (ktaps - 1)
    nt = t // tm
    rb = min(16, tm)
    state_pad = jnp.pad(state, ((0, 0), (off, 0), (0, 0)))
    n_e, n_c = len(extra_rows), len(consts)
    shifts = sorted({(off + k) % 8 for k in range(ktaps)} - {0})
    span = tm + CONV_HALO - 8

    def kern(x_ref, st_ref, w_ref, *rest):
        extra, cs = rest[:n_e], rest[n_e:n_e + n_c]
        o_ref, ns_ref, xpad, xs = rest[n_e + n_c:]
        ti = pl.program_id(1)

        @pl.when(ti == 0)
        def _():
            xpad[0:CONV_HALO, :] = st_ref[...]

        xpad[CONV_HALO:CONV_HALO + tm, :] = x_ref[...]
        for si, r in enumerate(shifts):
            xs[si, 0:span, :] = xpad[r:r + span, :]

        def chunk(c, carry):
            r0 = pl.multiple_of(c * rb, rb)
            acc = jnp.zeros((rb, d), F32)
            for k in range(ktaps):
                base, r = (off + k) // 8 * 8, (off + k) % 8
                rows = pl.ds(pl.multiple_of(r0 + base, 8), rb)
                src = xpad[rows, :] if r == 0 else xs[shifts.index(r), rows, :]
                acc = acc + w_ref[k:k + 1, :] * src
            o_ref[pl.ds(r0, rb), :] = epilogue(acc, [e[pl.ds(r0, rb), :] for e in extra], cs)
            return carry

        lax.fori_loop(0, tm // rb, chunk, 0)

        @pl.when(ti == nt - 1)
        def _():
            ns_ref[...] = xpad[valid_last:valid_last + CONV_HALO, :]

        @pl.when(ti < nt - 1)
        def _():
            xpad[0:CONV_HALO, :] = xpad[tm:tm + CONV_HALO, :]

    tile = pl.BlockSpec((None, tm, d), lambda bi, ti: (bi, ti, 0))
    per_seq = pl.BlockSpec((None, CONV_HALO, d), lambda bi, ti: (bi, 0, 0))
    const = lambda c: pl.BlockSpec(c.shape, lambda bi, ti, nd=c.ndim: (0,) * nd)
    out, ns = pl.pallas_call(
        kern, grid=(b, nt),
        in_specs=[tile, per_seq, const(w)] + [tile] * n_e + [const(c) for c in consts],
        out_specs=[tile, per_seq],
        out_shape=[jax.ShapeDtypeStruct((b, t, d), F32), jax.ShapeDtypeStruct((b, CONV_HALO, d), F32)],
        scratch_shapes=[pltpu.VMEM((CONV_HALO + tm, d), F32), pltpu.VMEM((len(shifts), CONV_HALO + tm, d), F32)],
        compiler_params=pltpu.CompilerParams(dimension_semantics=("parallel", "arbitrary"),
                                             vmem_limit_bytes=VMEM_LIMIT),
        name=name)(xin, state_pad, w, *extra_rows, *consts)
    return out, ns[:, off:, :]


def _conv_a_epilogue(acc, extra, cs):
    bdw, ln_g, ln_b = cs
    return _silu(_layernorm(acc + bdw[...], ln_g[...], ln_b[...]))


def _conv_b_epilogue(acc, extra, cs):
    return extra[0] * acc


def _neg_strict_lower_x2(n):
    m = -(jnp.arange(n)[:, None] > jnp.arange(n)[None, :]).astype(BF16)
    return jnp.concatenate([m, m], axis=0)


def _sb_block_weights(z, u2, mask):
    nlk = jnp.maximum(z, 0.0) + jnp.log(1.0 + jnp.exp(-jnp.abs(z)))
    if mask is not None:
        nlk = jnp.where(mask, nlk, 0.0)
    hi = nlk.astype(BF16)
    lo = (nlk - hi.astype(F32)).astype(BF16)
    suffix = jnp.dot(jnp.concatenate([hi, lo], axis=1), u2, preferred_element_type=F32)
    a = jnp.exp(z - nlk + suffix)
    if mask is not None:
        a = jnp.where(mask, a, 0.0)
    return a, jnp.sum(nlk, axis=-1, keepdims=True)


def _sb_prompt_call(qb, kb, vb, bias, blk):
    b, t, d = qb.shape
    nq = t // blk
    hps = SB_HEADS_PER_STEP
    hw = hps * SB_HEAD_DIM
    u2 = _neg_strict_lower_x2(blk)

    def kern(bias_ref, q_ref, k_ref, v_ref, u_ref, o_ref, acc_ref, carry_ref, hl_ref, d_ref, a_ref):
        hg, qi = pl.program_id(1), pl.program_id(2)
        tri = lax.broadcasted_iota(jnp.int32, (blk, blk), 1) < lax.broadcasted_iota(jnp.int32, (blk, blk), 0)
        lanes = lambda hh: slice(hh * SB_HEAD_DIM, (hh + 1) * SB_HEAD_DIM)
        rows = lambda kj: pl.ds(pl.multiple_of(kj * blk, blk), blk)

        def logits(hh, kj, mask):
            z = lax.dot_general(q_ref[:, lanes(hh)], k_ref[rows(kj), lanes(hh)], _NT, preferred_element_type=F32)
            z = z + bias_ref[hg * hps + hh]
            nlk = jnp.maximum(z, 0.0) + jnp.log(1.0 + jnp.exp(-jnp.abs(z)))
            if mask is not None:
                nlk = jnp.where(mask, nlk, 0.0)
            hi = nlk.astype(BF16)
            hl_ref[hh] = jnp.concatenate([hi, (nlk - hi.astype(F32)).astype(BF16)], axis=1)
            c = carry_ref[hh]
            dd = z - nlk + c
            d_ref[hh] = dd if mask is None else jnp.where(mask, dd, SB_MASKED_LOGIT)
            carry_ref[hh] = c - jnp.sum(nlk, axis=-1, keepdims=True)

        def weights(hh):
            suffix = jnp.dot(hl_ref[hh], u_ref[...], preferred_element_type=F32)
            a_ref[hh] = jnp.exp(d_ref[hh] + suffix).astype(BF16)

        def values(hh, kj):
            acc_ref[hh] += jnp.dot(a_ref[hh], v_ref[rows(kj), lanes(hh)], preferred_element_type=F32)

        acc_ref[...] = jnp.zeros_like(acc_ref)
        carry_ref[...] = jnp.zeros_like(carry_ref)
        a_ref[...] = jnp.zeros_like(a_ref)
        for hh in range(hps):
            logits(hh, qi, tri)

        def older(it, carry):
            kj = qi - 1 - it
            for hh in range(hps):
                values(hh, jnp.minimum(kj + 2, nq - 1))
            for hh in range(hps):
                weights(hh)
            for hh in range(hps):
                logits(hh, kj, None)
            return carry

        lax.fori_loop(0, qi, older, 0)
        for hh in range(hps):
            values(hh, min(1, nq - 1))
        for hh in range(hps):
            weights(hh)
        for hh in range(hps):
            values(hh, 0)
        o_ref[...] = jnp.concatenate([acc_ref[hh] for hh in range(hps)], axis=-1)

    tile = pl.BlockSpec((None, blk, hw), lambda bi, hg, qi: (bi, qi, hg))
    seq = pl.BlockSpec((None, t, hw), lambda bi, hg, qi: (bi, 0, hg))
    return pl.pallas_call(
        kern, grid=(b, SB_HEADS // hps, nq),
        in_specs=[pl.BlockSpec(memory_space=pltpu.SMEM), tile, seq, seq,
                  pl.BlockSpec(u2.shape, lambda bi, hg, qi: (0, 0))],
        out_specs=tile,
        out_shape=jax.ShapeDtypeStruct((b, t, d), F32),
        scratch_shapes=[pltpu.VMEM((hps, blk, SB_HEAD_DIM), F32), pltpu.VMEM((hps, blk, 1), F32),
                        pltpu.VMEM((hps, blk, 2 * blk), BF16), pltpu.VMEM((hps, blk, blk), F32),
                        pltpu.VMEM((hps, blk, blk), BF16)],
        compiler_params=pltpu.CompilerParams(dimension_semantics=("parallel", "parallel", "arbitrary"),
                                             vmem_limit_bytes=VMEM_LIMIT),
        name="sb_prompt")(bias, qb, kb, vb, u2)


def _sb_sample_call(q, k_new, v_new, cache_k, cache_v, page_table, bias, n_new):
    bsz, rows, d = q.shape
    n_pages = page_table.shape[1]
    grp = math.gcd(n_pages, SB_PAGES_PER_STEP)
    nr = n_new * SB_HEADS
    q_rep = jnp.repeat(q[:, :n_new], SB_HEADS, axis=1)
    bias_col = jnp.tile(bias, n_new)[:, None]
    u2 = _neg_strict_lower_x2(PAGE_SIZE)

    def kern(pt_ref, q_ref, kn_ref, vn_ref, *rest):
        kp_refs, vp_refs = rest[:grp], rest[grp:2 * grp]
        bias_ref, u_ref, o_ref, qbd, acc, carry = rest[2 * grp:]
        p = pl.program_id(1)
        row = lax.broadcasted_iota(jnp.int32, (nr, d), 0)
        lane = lax.broadcasted_iota(jnp.int32, (nr, d), 1)
        diag = (row % SB_HEADS) == (lane // SB_HEAD_DIM)

        def dense(page_ref):
            heads = [page_ref[pl.ds(h, PAGE_SIZE, stride=SB_HEADS), :] for h in range(SB_HEADS)]
            return jnp.concatenate(heads, axis=1).astype(BF16)

        def page(kp, vp, mask):
            z = lax.dot_general(qbd[...], kp, _NT, preferred_element_type=F32) + bias_ref[...]
            a, tot = _sb_block_weights(z, u_ref[...], mask)
            return jnp.dot(a.astype(BF16), vp, preferred_element_type=F32), tot

        def accumulate(parts):
            c = carry[...]
            upd = None
            for pv, tot in parts:
                term = jnp.exp(c) * pv
                upd = term if upd is None else upd + term
                c = c - tot
            acc[...] += upd
            carry[...] = c

        @pl.when(p == 0)
        def _():
            qbd[...] = jnp.where(diag, q_ref[...] * SB_SCALE, 0.0).astype(BF16)
            acc[...] = jnp.zeros_like(acc)
            carry[...] = jnp.zeros_like(carry)
            pad = jnp.zeros((PAGE_SIZE - rows, d), F32)
            kn = jnp.concatenate([kn_ref[...], pad], axis=0).astype(BF16)
            vn = jnp.concatenate([vn_ref[...], pad], axis=0).astype(BF16)
            key = lax.broadcasted_iota(jnp.int32, (nr, PAGE_SIZE), 1)
            tq = lax.broadcasted_iota(jnp.int32, (nr, PAGE_SIZE), 0) // SB_HEADS
            accumulate([page(kn, vn, (key < tq) & (key < n_new))])

        accumulate([page(dense(kp_refs[g]), dense(vp_refs[g]), None) for g in range(grp)])

        @pl.when(p == n_pages // grp - 1)
        def _():
            o_ref[...] = jnp.sum(jnp.where(diag, acc[...], 0.0).reshape(n_new, SB_HEADS, d), axis=1)

    seq = lambda r: pl.BlockSpec((None, r, d), lambda bi, p, pt: (bi, 0, 0))
    pagespec = lambda g: pl.BlockSpec((None, PAGE_SIZE * SB_HEADS, SB_HEAD_DIM),
                                      lambda bi, p, pt: (pt[bi, n_pages - 1 - (p * grp + g)], 0, 0))
    const = lambda c: pl.BlockSpec(c.shape, lambda bi, p, pt: (0, 0))
    pages = [pagespec(g) for g in range(grp)]
    out = pl.pallas_call(
        kern,
        grid_spec=pltpu.PrefetchScalarGridSpec(
            num_scalar_prefetch=1, grid=(bsz, n_pages // grp),
            in_specs=[seq(nr), seq(rows), seq(rows)] + pages + pages + [const(bias_col), const(u2)],
            out_specs=seq(n_new),
            scratch_shapes=[pltpu.VMEM((nr, d), BF16), pltpu.VMEM((nr, d), F32), pltpu.VMEM((nr, 1), F32)]),
        out_shape=jax.ShapeDtypeStruct((bsz, n_new, d), F32),
        compiler_params=pltpu.CompilerParams(dimension_semantics=("parallel", "arbitrary"),
                                             vmem_limit_bytes=VMEM_LIMIT),
        name="sb_sample")(page_table, q_rep, k_new, v_new, *([cache_k] * grp), *([cache_v] * grp), bias_col, u2)
    return jnp.pad(out, ((0, 0), (0, rows - n_new), (0, 0)))


def _ret_tables(c, valid):
    log_gamma = jnp.log(1.0 - jnp.power(2.0, -5.0 - jnp.arange(RET_HEADS, dtype=F32)))
    idx = jnp.arange(c, dtype=F32)
    diff = idx[:, None] - idx[None, :]
    intra = jnp.where(diff >= 0, jnp.exp(jnp.maximum(diff, 0.0)[None] * log_gamma[:, None, None]), 0.0)
    q_decay = jnp.exp((idx + 1.0)[None, :] * log_gamma[:, None])[..., None]
    k_decay = jnp.where(idx < valid, jnp.exp((valid - 1.0 - idx)[None, :] * log_gamma[:, None]), 0.0)[..., None]
    chunk_decay = jnp.exp(valid * log_gamma)
    return intra, q_decay, k_decay, chunk_decay


def _retention_call(q, k, v, g, s0, gn_g, c, valid):
    b, t, _ = q.shape
    nc = t // c
    intra, q_decay, k_decay, chunk_decay = _ret_tables(c, valid)

    def kern(cd_ref, q_ref, k_ref, v_ref, g_ref, s0_ref, gn_ref, in_ref, qd_ref, kd_ref, y_ref, s_ref):
        h, ci = pl.program_id(1), pl.program_id(2)

        @pl.when(ci == 0)
        def _():
            s_ref[...] = s0_ref[...]

        state = s_ref[...]
        qc, kc = q_ref[...].astype(BF16), k_ref[...]
        vc = v_ref[...].astype(BF16)
        scores = lax.dot_general(qc, kc.astype(BF16), _NT, preferred_element_type=F32) * in_ref[...]
        o = (_dot(scores, vc) + _dot(qc, state.astype(BF16)) * qd_ref[...])
        s_ref[...] = state * cd_ref[h] + _dot((kc * kd_ref[...]).T, vc)
        y_ref[...] = _silu(g_ref[...]) * _layernorm(o, gn_ref[...])

    qk = pl.BlockSpec((None, c, RET_DK), lambda bi, h, ci: (bi, ci, h))
    vg = pl.BlockSpec((None, c, RET_DV), lambda bi, h, ci: (bi, ci, h))
    st = pl.BlockSpec((None, None, RET_DK, RET_DV), lambda bi, h, ci: (bi, h, 0, 0))
    per_head = lambda *shape: pl.BlockSpec((None,) + shape, lambda bi, h, ci: (h,) + (0,) * len(shape))
    return pl.pallas_call(
        kern, grid=(b, RET_HEADS, nc),
        in_specs=[pl.BlockSpec(memory_space=pltpu.SMEM), qk, qk, vg, vg, st, per_head(1, RET_DV),
                  per_head(c, c), per_head(c, 1), per_head(c, 1)],
        out_specs=[vg, st],
        out_shape=[jax.ShapeDtypeStruct((b, t, RET_HEADS * RET_DV), F32),
                   jax.ShapeDtypeStruct((b, RET_HEADS, RET_DK, RET_DV), F32)],
        compiler_params=pltpu.CompilerParams(dimension_semantics=("parallel", "parallel", "arbitrary"),
                                             vmem_limit_bytes=VMEM_LIMIT),
        name="retention")(chunk_decay, q, k, v, g, s0, gn_g, intra, q_decay, k_decay)


def _mem_attend_sample_call(q, cache_k, cache_v, layer):
    bsz, rows, d = q.shape
    m = cache_k.shape[2]

    def kern(q_ref, k_ref, v_ref, o_ref):
        qv = q_ref[...]
        heads = []
        for hd in range(MEM_HEADS):
            sl = slice(hd * MEM_HEAD_DIM, (hd + 1) * MEM_HEAD_DIM)
            heads.append(_mem_attend(qv[:, sl], k_ref[:, sl].astype(BF16), v_ref[:, sl].astype(BF16)))
        o_ref[...] = jnp.concatenate(heads, axis=-1)

    seq = pl.BlockSpec((None, rows, d), lambda bi: (bi, 0, 0))
    kv = pl.BlockSpec((None, None, m, d), lambda bi: (layer, bi, 0, 0))
    return pl.pallas_call(
        kern, grid=(bsz,), in_specs=[seq, kv, kv], out_specs=seq,
        out_shape=jax.ShapeDtypeStruct((bsz, rows, d), F32),
        compiler_params=pltpu.CompilerParams(dimension_semantics=("parallel",), vmem_limit_bytes=VMEM_LIMIT),
        name="mem_attend_sample")(q, cache_k, cache_v)


def _rope_tables(pos):
    half = RET_DK // 2
    inv_freq = jnp.power(ROPE_BASE, -jnp.arange(0, RET_DK, 2, dtype=F32) / RET_DK)
    ang = pos.astype(F32)[:, None] * inv_freq[None, :]
    return jnp.cos(ang), jnp.sin(ang)


def _trunk(x, t_valid, pos0, p, states, sb_past, mem):
    b, t, d = x.shape
    m = b * t
    tm = _pick_tile(t, ROW_TILE)
    tpb = t // tm
    if mem[0] == 'sample':
        tm, tpb = m, 1
    state_a, state_b, state_r = states
    row = lambda r: r.reshape(m, -1)
    vec = lambda g: g.reshape(1, -1)
    xf = row(x)
    ct = _pick_tile(t, CONV_TILE)
    new = {}
    for i in range(4):
        g_mix = vec(p['norm_mix'][i])
        if i == 0:
            (gl,) = _row_call("pre_conv_a", _pre_conv_a, m, tm, [xf], [(d, F32)],
                              consts=[g_mix, p['cva_w1'], vec(p['cva_b1'])])
            mix, new['a'] = _conv_call("conv_a", gl.reshape(b, t, d), state_a, p['cva_wdw'], _conv_a_epilogue,
                                       ct, t_valid if t_valid < t else ct,
                                       consts=[vec(p['cva_bdw']), vec(p['cva_ln_g']), vec(p['cva_ln_b'])])
            wo = p['cva_w2']
        elif i == 1:
            bg, cu = _row_call("pre_conv_b", _pre_conv_b, m, tm, [xf], [(d, F32)] * 2, consts=[g_mix, p['scb_win']])
            mix, new['b'] = _conv_call("conv_b", cu.reshape(b, t, d), state_b, p['scb_wconv'], _conv_b_epilogue,
                                       ct, t_valid if t_valid < t else ct,
                                       extra_rows=[bg.reshape(b, t, d)])
            wo = p['scb_wout']
        elif i == 2:
            if sb_past is None:
                q, k, v, qb, kb, vb = _row_call("pre_sb", _pre_sb, m, tm, [xf], [(d, F32)] * 3 + [(d, BF16)] * 3,
                                                consts=[g_mix, p['sb_wqkv']])
                mix = _sb_prompt_call(qb.reshape(b, t, d), kb.reshape(b, t, d), vb.reshape(b, t, d), p['sb_bias'],
                                      _pick_tile(t, SB_BLOCK))
            else:
                q, k, v = _row_call("pre_sb", _pre_sb, m, tm, [xf], [(d, F32)] * 3, consts=[g_mix, p['sb_wqkv']])
                cache_k, cache_v, page_table = sb_past
                mix = _sb_sample_call(q.reshape(b, t, d), k.reshape(b, t, d), v.reshape(b, t, d), cache_k, cache_v,
                                      page_table, p['sb_bias'], t_valid)
            new['k'], new['v'] = k.reshape(b, t, d), v.reshape(b, t, d)
            wo = p['sb_wo']
        else:
            cos, sin = _rope_tables(pos0 + jnp.arange(t))
            if mem[0] == 'sample':
                cos, sin = jnp.tile(cos, (b, 1)), jnp.tile(sin, (b, 1))
            q, k, v, g = _row_call("pre_ret", _pre_ret, m, tm, [xf], [(d, F32)] * 2 + [(2 * d, F32)] * 2,
                                   consts=[g_mix, p['ret_wqkvg']], prows=[cos, sin], tiles_per_batch=tpb)
            c = _pick_tile(t, RET_CHUNK)
            mix, new['r'] = _retention_call(q.reshape(b, t, d), k.reshape(b, t, d), v.reshape(b, t, 2 * d),
                                            g.reshape(b, t, 2 * d), state_r, p['ret_gn_g'].reshape(RET_HEADS, 1, RET_DV),
                                            c, min(c, t_valid))
            wo = p['ret_wo']
        mix = row(mix)
        g_cross, qn = vec(p['norm_cross'][i]), vec(p['xa_qnorm'][i])
        mlp_consts = [vec(p['norm_mlp'][i]), p['mlp_w1'][i], p['mlp_w2'][i]]
        if mem[0] == 'prompt':
            (x2,) = _row_call("post_cross", _post_cross_prompt, m, tm, [xf, mix], [(d, F32)],
                              consts=[wo, g_cross, p['xa_wq'][i], qn, p['xa_wo'][i]],
                              bconsts=[mem[1][i], mem[2][i]], tiles_per_batch=tpb)
            (xf,) = _row_call("post_mlp", _post_mlp, m, tm, [x2], [(d, F32)], consts=mlp_consts)
        else:
            x1, qm = _row_call("post_q", _post_q_sample, m, tm, [xf, mix], [(d, F32)] * 2,
                               consts=[wo, g_cross, p['xa_wq'][i], qn])
            o = _mem_attend_sample_call(qm.reshape(b, t, d), mem[1], mem[2], i)
            (xf,) = _row_call("post_xo_mlp", _post_xo_mlp, m, tm, [x1, row(o)], [(d, F32)],
                              consts=[p['xa_wo'][i]] + mlp_consts)
    return xf.reshape(b, t, d), new


def kernel(x_prompt, x_sample, mem_prompt, state_conv_a, state_conv_b, cache_sb_k, cache_sb_v, page_table, state_ret, cache_mem_k, cache_mem_v, norm_mix, norm_cross, norm_mem, norm_mlp, xa_wq, xa_wk, xa_wv, xa_wo, xa_qnorm, xa_knorm, mlp_w1, mlp_w2, cva_w1, cva_b1, cva_wdw, cva_bdw, cva_ln_g, cva_ln_b, cva_w2, scb_win, scb_wconv, scb_wout, sb_wqkv, sb_wo, sb_bias, ret_wqkvg, ret_gn_g, ret_wo):
    bp, seq, d = x_prompt.shape
    bs, dec_seq, _ = x_sample.shape
    depth = norm_mix.shape[0]
    assert depth == 4 and d == D_MODEL and dec_seq <= SAMPLE_ROWS
    n_mem = mem_prompt.shape[1]
    past_len = page_table.shape[1] * PAGE_SIZE
    wb = lambda w: w.astype(BF16)
    p = {
        'norm_mix': norm_mix, 'norm_cross': norm_cross, 'norm_mlp': norm_mlp,
        'xa_wq': wb(xa_wq), 'xa_wo': wb(xa_wo), 'xa_qnorm': xa_qnorm, 'mlp_w1': wb(mlp_w1), 'mlp_w2': wb(mlp_w2),
        'cva_w1': wb(cva_w1[0]), 'cva_b1': cva_b1[0], 'cva_wdw': cva_wdw[0], 'cva_bdw': cva_bdw[0],
        'cva_ln_g': cva_ln_g[0], 'cva_ln_b': cva_ln_b[0], 'cva_w2': wb(cva_w2[0]),
        'scb_win': wb(scb_win[0]), 'scb_wconv': scb_wconv[0], 'scb_wout': wb(scb_wout[0]),
        'sb_wqkv': wb(sb_wqkv[0]), 'sb_wo': wb(sb_wo[0]), 'sb_bias': sb_bias[0],
        'ret_wqkvg': wb(ret_wqkvg[0]), 'ret_gn_g': ret_gn_g[0], 'ret_wo': wb(ret_wo[0]),
    }

    mk, mv, mkb, mvb = _mem_kv_call(mem_prompt.reshape(bp * n_mem, d), norm_mem.reshape(depth, 1, d), wb(xa_wk),
                                    wb(xa_wv), xa_knorm.reshape(depth, 1, MEM_HEAD_DIM))
    per_seq = lambda a: a.reshape(depth, bp, n_mem, d)
    zeros = lambda *s: jnp.zeros(s, F32)
    y_p, new_p = _trunk(x_prompt, seq, 0, p,
                        (zeros(bp, CONV_A_WIDTH - 1, d), zeros(bp, CONV_B_WIDTH - 1, d),
                         zeros(bp, RET_HEADS, RET_DK, RET_DV)),
                        None, ('prompt', per_seq(mkb), per_seq(mvb)))

    xs = jnp.pad(x_sample, ((0, 0), (0, SAMPLE_ROWS - dec_seq), (0, 0)))
    pool = cache_sb_k.shape[1]
    y_s, new_s = _trunk(xs, dec_seq, past_len, p, (state_conv_a[0], state_conv_b[0], state_ret[0]),
                        (cache_sb_k[0].reshape(pool, PAGE_SIZE * SB_HEADS, SB_HEAD_DIM),
                         cache_sb_v[0].reshape(pool, PAGE_SIZE * SB_HEADS, SB_HEAD_DIM),
                         page_table),
                        ('sample', cache_mem_k.reshape(depth, bs, n_mem, d), cache_mem_v.reshape(depth, bs, n_mem, d)))

    heads = lambda a, n: a[:, :n].reshape(1, a.shape[0], n, SB_HEADS, SB_HEAD_DIM)
    mem_out = lambda a: a.reshape(depth, bp, n_mem, MEM_HEADS, MEM_HEAD_DIM)
    return (y_p, y_s[:, :dec_seq], new_p['a'][None], new_s['a'][None], new_p['b'][None], new_s['b'][None],
            heads(new_p['k'], seq), heads(new_p['v'], seq), heads(new_s['k'], dec_seq), heads(new_s['v'], dec_seq),
            new_p['r'][None], new_s['r'][None], mem_out(mk), mem_out(mv))
```

```python
import functools
import math

import jax
import jax.numpy as jnp
from jax import lax
from jax.experimental import pallas as pl
from jax.experimental.pallas import tpu as pltpu

F32, BF16 = jnp.float32, jnp.bfloat16

D_MODEL = 1024
NORM_EPS = 1e-6
PAGE_SIZE = 128
CONV_A_WIDTH, CONV_B_WIDTH = 31, 3
SB_HEADS, SB_HEAD_DIM = 16, 64
SB_SCALE = SB_HEAD_DIM ** -0.5
RET_HEADS, RET_DK, RET_DV = 4, 256, 512
ROPE_BASE = 10000.0
MEM_HEADS, MEM_HEAD_DIM = 4, 256
MEM_SCALE = MEM_HEAD_DIM ** -0.5
D_FF = 4 * D_MODEL

CONV_HALO = 32
SAMPLE_ROWS = 8
VMEM_LIMIT = 56 * 1024 * 1024
ROW_TILE = 512
CONV_TILE = 256
SB_BLOCK = 256
SB_HEADS_PER_STEP = 4
SB_MASKED_LOGIT = -1e30
SB_PAGES_PER_STEP = 4
RET_CHUNK = 256

_NT = (((1,), (1,)), ((), ()))


def _dot(a, b):
    return jnp.dot(a.astype(BF16), b, preferred_element_type=F32)


def _dot_nt(a, b):
    return lax.dot_general(a.astype(BF16), b, _NT, preferred_element_type=F32)


def _rms(x, g):
    return x * lax.rsqrt(jnp.mean(x * x, axis=-1, keepdims=True) + NORM_EPS) * g


def _layernorm(x, g, b=None):
    xc = x - jnp.mean(x, axis=-1, keepdims=True)
    y = xc * lax.rsqrt(jnp.mean(xc * xc, axis=-1, keepdims=True) + NORM_EPS) * g
    return y if b is None else y + b


def _silu(x):
    return x * jax.nn.sigmoid(x)


def _pick_tile(n, pref):
    t = min(n, pref)
    while n % t:
        t -= 8
    return t


def _row_call(name, body, m, tm, rows, outs, consts=(), bconsts=(), prows=(), tiles_per_batch=1, outs_t=()):
    in_specs, args = [], []
    for r in rows:
        arr, width, cb = r if isinstance(r, tuple) else (r, r.shape[1], 0)
        in_specs.append(pl.BlockSpec((tm, width), lambda i, cb=cb: (i, cb)))
        args.append(arr)
    for p in prows:
        in_specs.append(pl.BlockSpec((tm, p.shape[1]), lambda i: (i % tiles_per_batch, 0)))
        args.append(p)
    for c in bconsts:
        in_specs.append(pl.BlockSpec((None,) + c.shape[1:],
                                     lambda i, nd=c.ndim: (i // tiles_per_batch,) + (0,) * (nd - 1)))
        args.append(c)
    for c in consts:
        in_specs.append(pl.BlockSpec(c.shape, lambda i, nd=c.ndim: (0,) * nd, pipeline_mode=pl.Buffered(1)))
        args.append(c)
    n_r, n_p, n_b, n_c = len(rows), len(prows), len(bconsts), len(consts)

    def kern(*refs):
        a, b, c, d = n_r, n_r + n_p, n_r + n_p + n_b, n_r + n_p + n_b + n_c
        body(refs[:a], refs[a:b], refs[b:c], refs[c:d], refs[d:])

    t = tm * tiles_per_batch
    fm_spec = lambda f: pl.BlockSpec((None, f, tm), lambda i: (i // tiles_per_batch, 0, i % tiles_per_batch))
    return pl.pallas_call(
        kern, grid=(m // tm,), in_specs=in_specs,
        out_specs=[pl.BlockSpec((tm, w), lambda i: (i, 0)) for w, _ in outs] + [fm_spec(f) for f, _ in outs_t],
        out_shape=[jax.ShapeDtypeStruct((m, w), dt) for w, dt in outs]
        + [jax.ShapeDtypeStruct((m // t, f, t), dt) for f, dt in outs_t],
        compiler_params=pltpu.CompilerParams(dimension_semantics=("parallel",), vmem_limit_bytes=VMEM_LIMIT),
        name=name)(*args)


def _mem_q(x1, g_cross, wq_ref, qn):
    q = _dot(_rms(x1, g_cross), wq_ref[...])
    heads = []
    for hd in range(MEM_HEADS):
        qh = q[:, hd * MEM_HEAD_DIM:(hd + 1) * MEM_HEAD_DIM]
        heads.append(qh * lax.rsqrt(jnp.mean(qh * qh, axis=-1, keepdims=True) + NORM_EPS) * qn)
    return heads


def _mem_attend(qh, mk_h, mv_h):
    s = _dot_nt(qh, mk_h) * MEM_SCALE
    p = jnp.exp(s - jnp.max(s, axis=-1, keepdims=True))
    a = p / jnp.sum(p, axis=-1, keepdims=True)
    return _dot(a, mv_h)


def _mlp(x2, g_mlp, w1_ref, w2_ref):
    h = _rms(x2, g_mlp).astype(BF16)
    acc = x2
    for c in range(D_FF // D_MODEL):
        sl = slice(c * D_MODEL, (c + 1) * D_MODEL)
        u = jnp.dot(h, w1_ref[:, sl], preferred_element_type=F32)
        u = jnp.square(jnp.maximum(u, 0.0))
        acc = acc + _dot(u, w2_ref[sl, :])
    return acc


def _rotary(x, cos, sin):
    half = RET_DK // 2
    parts = []
    for hd in range(RET_HEADS):
        x1 = x[:, hd * RET_DK: hd * RET_DK + half]
        x2 = x[:, hd * RET_DK + half: (hd + 1) * RET_DK]
        parts += [x1 * cos - x2 * sin, x1 * sin + x2 * cos]
    return jnp.concatenate(parts, axis=-1)


def _pre_conv_a(rows, prows, bcs, cs, outs):
    g_mix, w1, b1 = cs
    h = _rms(rows[0][...], g_mix[...])
    a = _dot(h, w1[:, :D_MODEL]) + b1[:, :D_MODEL]
    gate = _dot(h, w1[:, D_MODEL:]) + b1[:, D_MODEL:]
    outs[0][...] = a * jax.nn.sigmoid(gate)


def _pre_conv_b(rows, prows, bcs, cs, outs):
    g_mix, win = cs
    h = _rms(rows[0][...], g_mix[...])
    outs[0][...] = _dot(h, win[:, :D_MODEL])
    outs[1][...] = _dot(h, win[:, D_MODEL:2 * D_MODEL]) * _dot(h, win[:, 2 * D_MODEL:])


def _pre_sb(rows, prows, bcs, cs, outs):
    g_mix, wqkv = cs
    h = _rms(rows[0][...], g_mix[...])
    q = _dot(h, wqkv[:, :D_MODEL])
    k = _dot(h, wqkv[:, D_MODEL:2 * D_MODEL])
    v = _dot(h, wqkv[:, 2 * D_MODEL:])
    outs[0][...] = q
    outs[1][...] = k
    outs[2][...] = v


def _pre_sb_prompt(rows, prows, bcs, cs, outs):
    g_mix, wq, wkt, wvt = cs
    h = _rms(rows[0][...], g_mix[...]).astype(BF16)
    outs[0][...] = (jnp.dot(h, wq[...], preferred_element_type=F32) * SB_SCALE).astype(BF16)
    kt = lax.dot_general(wkt[...], h, _NT, preferred_element_type=F32)
    vt = lax.dot_general(wvt[...], h, _NT, preferred_element_type=F32)
    outs[1][...] = kt
    outs[2][...] = vt
    outs[3][...] = kt.astype(BF16)
    outs[4][...] = vt.astype(BF16)


def _pre_ret(rows, prows, bcs, cs, outs):
    g_mix, w = cs
    cos, sin = prows[0][...], prows[1][...]
    h = _rms(rows[0][...], g_mix[...])
    d = D_MODEL
    outs[0][...] = _rotary(_dot(h, w[:, :d]), cos, sin)
    outs[1][...] = _rotary(_dot(h, w[:, d:2 * d]), cos, sin) * (RET_DK ** -0.5)
    outs[2][...] = _dot(h, w[:, 2 * d:4 * d])
    outs[3][...] = _dot(h, w[:, 4 * d:])


def _post_cross_prompt(rows, prows, bcs, cs, outs):
    x_ref, mix_ref = rows
    mk_ref, mv_ref = bcs
    wo, g_cross, wq, qn, xwo = cs
    x1 = x_ref[...] + _dot(mix_ref[...], wo[...])
    acc = x1
    for hd, qh in enumerate(_mem_q(x1, g_cross[...], wq, qn[...])):
        sl = slice(hd * MEM_HEAD_DIM, (hd + 1) * MEM_HEAD_DIM)
        acc = acc + _dot(_mem_attend(qh, mk_ref[:, sl], mv_ref[:, sl]), xwo[sl, :])
    outs[0][...] = acc


def _post_q_sample(rows, prows, bcs, cs, outs):
    x_ref, mix_ref = rows
    wo, g_cross, wq, qn = cs
    x1 = x_ref[...] + _dot(mix_ref[...], wo[...])
    outs[0][...] = x1
    outs[1][...] = jnp.concatenate(_mem_q(x1, g_cross[...], wq, qn[...]), axis=-1)


def _post_mlp(rows, prows, bcs, cs, outs):
    g_mlp, w1, w2 = cs
    outs[0][...] = _mlp(rows[0][...], g_mlp[...], w1, w2)


def _post_xo_mlp(rows, prows, bcs, cs, outs):
    xwo, g_mlp, w1, w2 = cs
    x2 = rows[0][...] + _dot(rows[1][...], xwo[...])
    outs[0][...] = _mlp(x2, g_mlp[...], w1, w2)


def _mem_kv_call(mem, norm_mem, wk, wv, knorm):
    depth = wk.shape[0]
    m = mem.shape[0]

    def kern(mem_ref, g_ref, wk_ref, wv_ref, kn_ref, k_ref, v_ref, kb_ref, vb_ref):
        mn = _rms(mem_ref[...], g_ref[...])
        k = _dot(mn, wk_ref[...])
        kn = kn_ref[...]
        heads = []
        for hd in range(MEM_HEADS):
            kh = k[:, hd * MEM_HEAD_DIM:(hd + 1) * MEM_HEAD_DIM]
            heads.append(kh * lax.rsqrt(jnp.mean(kh * kh, axis=-1, keepdims=True) + NORM_EPS) * kn)
        k = jnp.concatenate(heads, axis=-1)
        v = _dot(mn, wv_ref[...])
        k_ref[...] = k
        v_ref[...] = v
        kb_ref[...] = k.astype(BF16)
        vb_ref[...] = v.astype(BF16)

    lay = lambda *shape: pl.BlockSpec((None,) + shape, lambda i: (i,) + (0,) * len(shape))
    return pl.pallas_call(
        kern, grid=(depth,),
        in_specs=[pl.BlockSpec(mem.shape, lambda i: (0, 0)), lay(1, D_MODEL), lay(D_MODEL, D_MODEL),
                  lay(D_MODEL, D_MODEL), lay(1, MEM_HEAD_DIM)],
        out_specs=[lay(m, D_MODEL)] * 4,
        out_shape=[jax.ShapeDtypeStruct((depth, m, D_MODEL), F32)] * 2
        + [jax.ShapeDtypeStruct((depth, m, D_MODEL), BF16)] * 2,
        compiler_params=pltpu.CompilerParams(dimension_semantics=("parallel",), vmem_limit_bytes=VMEM_LIMIT),
        name="mem_kv")(mem, norm_mem, wk, wv, knorm)


def _conv_call(name, xin, state, w, epilogue, tm, valid_last, extra_rows=(), consts=()):
    b, t, d = xin.shape
    ktaps = w.shape[0]
    off = CONV_HALO - (ktaps - 1)
    nt = t // tm
    rb = min(16, tm)
    state_pad = jnp.pad(state, ((0, 0), (off, 0), (0, 0)))
    n_e, n_c = len(extra_rows), len(consts)
    shifts = sorted({(off + k) % 8 for k in range(ktaps)} - {0})
    span = tm + CONV_HALO - 8

    def kern(x_ref, st_ref, w_ref, *rest):
        extra, cs = rest[:n_e], rest[n_e:n_e + n_c]
        o_ref, ns_ref, xpad, xs = rest[n_e + n_c:]
        ti = pl.program_id(1)

        @pl.when(ti == 0)
        def _():
            xpad[0:CONV_HALO, :] = st_ref[...]

        xpad[CONV_HALO:CONV_HALO + tm, :] = x_ref[...]
        for si, r in enumerate(shifts):
            xs[si, 0:span, :] = xpad[r:r + span, :]

        def chunk(c, carry):
            r0 = pl.multiple_of(c * rb, rb)
            acc = jnp.zeros((rb, d), F32)
            for k in range(ktaps):
                base, r = (off + k) // 8 * 8, (off + k) % 8
                rows = pl.ds(pl.multiple_of(r0 + base, 8), rb)
                src = xpad[rows, :] if r == 0 else xs[shifts.index(r), rows, :]
                acc = acc + w_ref[k:k + 1, :] * src
            o_ref[pl.ds(r0, rb), :] = epilogue(acc, [e[pl.ds(r0, rb), :] for e in extra], cs)
            return carry

        lax.fori_loop(0, tm // rb, chunk, 0)

        @pl.when(ti == nt - 1)
        def _():
            ns_ref[...] = xpad[valid_last:valid_last + CONV_HALO, :]

        @pl.when(ti < nt - 1)
        def _():
            xpad[0:CONV_HALO, :] = xpad[tm:tm + CONV_HALO, :]

    tile = pl.BlockSpec((None, tm, d), lambda bi, ti: (bi, ti, 0))
    per_seq = pl.BlockSpec((None, CONV_HALO, d), lambda bi, ti: (bi, 0, 0))
    const = lambda c: pl.BlockSpec(c.shape, lambda bi, ti, nd=c.ndim: (0,) * nd)
    out, ns = pl.pallas_call(
        kern, grid=(b, nt),
        in_specs=[tile, per_seq, const(w)] + [tile] * n_e + [const(c) for c in consts],
        out_specs=[tile, per_seq],
        out_shape=[jax.ShapeDtypeStruct((b, t, d), F32), jax.ShapeDtypeStruct((b, CONV_HALO, d), F32)],
        scratch_shapes=[pltpu.VMEM((CONV_HALO + tm, d), F32), pltpu.VMEM((len(shifts), CONV_HALO + tm, d), F32)],
        compiler_params=pltpu.CompilerParams(dimension_semantics=("parallel", "arbitrary"),
                                             vmem_limit_bytes=VMEM_LIMIT),
        name=name)(xin, state_pad, w, *extra_rows, *consts)
    return out, ns[:, off:, :]


def _conv_a_epilogue(acc, extra, cs):
    bdw, ln_g, ln_b = cs
    return _silu(_layernorm(acc + bdw[...], ln_g[...], ln_b[...]))


def _conv_b_epilogue(acc, extra, cs):
    return extra[0] * acc


def _neg_strict_lower_x2(n):
    m = -(jnp.arange(n)[:, None] > jnp.arange(n)[None, :]).astype(BF16)
    return jnp.concatenate([m, m], axis=0)


def _sb_block_weights(z, u2, mask):
    nlk = jnp.maximum(z, 0.0) + jnp.log(1.0 + jnp.exp(-jnp.abs(z)))
    if mask is not None:
        nlk = jnp.where(mask, nlk, 0.0)
    hi = nlk.astype(BF16)
    lo = (nlk - hi.astype(F32)).astype(BF16)
    suffix = jnp.dot(jnp.concatenate([hi, lo], axis=1), u2, preferred_element_type=F32)
    a = jnp.exp(z - nlk + suffix)
    if mask is not None:
        a = jnp.where(mask, a, 0.0)
    return a, jnp.sum(nlk, axis=-1, keepdims=True)


def _sb_prompt_call(qb, kb, vb, bias, blk):
    b, t, d = qb.shape
    nq = t // blk
    hps = SB_HEADS_PER_STEP
    hw = hps * SB_HEAD_DIM
    u2 = _neg_strict_lower_x2(blk)

    def kern(bias_ref, q_ref, k_ref, v_ref, u_ref, o_ref, acc_ref, carry_ref, hl_ref, d_ref, a_ref):
        hg, qi = pl.program_id(1), pl.program_id(2)
        tri = lax.broadcasted_iota(jnp.int32, (blk, blk), 1) < lax.broadcasted_iota(jnp.int32, (blk, blk), 0)
        lanes = lambda hh: slice(hh * SB_HEAD_DIM, (hh + 1) * SB_HEAD_DIM)
        cols = lambda kj: pl.ds(pl.multiple_of(kj * blk, blk), blk)

        def logits(hh, kj, mask):
            z = jnp.dot(q_ref[:, lanes(hh)], k_ref[lanes(hh), cols(kj)], preferred_element_type=F32)
            z = z + bias_ref[hg * hps + hh]
            nlk = jnp.maximum(z, 0.0) + jnp.log(1.0 + jnp.exp(-jnp.abs(z)))
            if mask is not None:
                nlk = jnp.where(mask, nlk, 0.0)
            hi = nlk.astype(BF16)
            hl_ref[hh] = jnp.concatenate([hi, (nlk - hi.astype(F32)).astype(BF16)], axis=1)
            c = carry_ref[hh]
            dd = z - nlk + c
            d_ref[hh] = dd if mask is None else jnp.where(mask, dd, SB_MASKED_LOGIT)
            carry_ref[hh] = c - jnp.sum(nlk, axis=-1, keepdims=True)

        def weights(hh):
            suffix = jnp.dot(hl_ref[hh], u_ref[...], preferred_element_type=F32)
            a_ref[hh] = jnp.exp(d_ref[hh] + suffix).astype(BF16)

        def values(hh, kj):
            acc_ref[hh] += lax.dot_general(a_ref[hh], v_ref[lanes(hh), cols(kj)], _NT, preferred_element_type=F32)

        acc_ref[...] = jnp.zeros_like(acc_ref)
        carry_ref[...] = jnp.zeros_like(carry_ref)
        a_ref[...] = jnp.zeros_like(a_ref)
        for hh in range(hps):
            logits(hh, qi, tri)

        def older(it, carry):
            kj = qi - 1 - it
            for hh in range(hps):
                values(hh, jnp.minimum(kj + 2, nq - 1))
            for hh in range(hps):
                weights(hh)
            for hh in range(hps):
                logits(hh, kj, None)
            return carry

        lax.fori_loop(0, qi, older, 0)
        for hh in range(hps):
            values(hh, min(1, nq - 1))
        for hh in range(hps):
            weights(hh)
        for hh in range(hps):
            values(hh, 0)
        o_ref[...] = jnp.concatenate([acc_ref[hh] for hh in range(hps)], axis=-1)

    tile = pl.BlockSpec((None, blk, hw), lambda bi, hg, qi: (bi, qi, hg))
    seq = pl.BlockSpec((None, hw, t), lambda bi, hg, qi: (bi, hg, 0))
    return pl.pallas_call(
        kern, grid=(b, SB_HEADS // hps, nq),
        in_specs=[pl.BlockSpec(memory_space=pltpu.SMEM), tile, seq, seq,
                  pl.BlockSpec(u2.shape, lambda bi, hg, qi: (0, 0))],
        out_specs=tile,
        out_shape=jax.ShapeDtypeStruct((b, t, d), F32),
        scratch_shapes=[pltpu.VMEM((hps, blk, SB_HEAD_DIM), F32), pltpu.VMEM((hps, blk, 1), F32),
                        pltpu.VMEM((hps, blk, 2 * blk), BF16), pltpu.VMEM((hps, blk, blk), F32),
                        pltpu.VMEM((hps, blk, blk), BF16)],
        compiler_params=pltpu.CompilerParams(dimension_semantics=("parallel", "parallel", "arbitrary"),
                                             vmem_limit_bytes=VMEM_LIMIT),
        name="sb_prompt")(bias, qb, kb, vb, u2)


def _sb_sample_call(q, k_new, v_new, cache_kt, cache_vt, page_table, bias, n_new):
    bsz, rows, d = q.shape
    n_pages = page_table.shape[1]
    grp = math.gcd(n_pages, SB_PAGES_PER_STEP)
    nr = n_new * SB_HEADS
    q_rep = jnp.repeat(q[:, :n_new], SB_HEADS, axis=1)
    bias_col = jnp.tile(bias, n_new)[:, None]
    u2 = _neg_strict_lower_x2(PAGE_SIZE)
    as_page = lambda x: jnp.swapaxes(jnp.pad(x, ((0, 0), (0, PAGE_SIZE - rows), (0, 0))), 1, 2)

    def kern(pt_ref, q_ref, kn_ref, vn_ref, *rest):
        kp_refs, vp_refs = rest[:grp], rest[grp:2 * grp]
        bias_ref, u_ref, o_ref, qbd, acc, carry = rest[2 * grp:]
        p = pl.program_id(1)
        row = lax.broadcasted_iota(jnp.int32, (nr, d), 0)
        lane = lax.broadcasted_iota(jnp.int32, (nr, d), 1)
        diag = (row % SB_HEADS) == (lane // SB_HEAD_DIM)

        def page(kt_ref, vt_ref, mask):
            z = jnp.dot(qbd[...], kt_ref[...].astype(BF16), preferred_element_type=F32) + bias_ref[...]
            a, tot = _sb_block_weights(z, u_ref[...], mask)
            return _dot_nt(a, vt_ref[...].astype(BF16)), tot

        def accumulate(parts):
            c = carry[...]
            upd = None
            for pv, tot in parts:
                term = jnp.exp(c) * pv
                upd = term if upd is None else upd + term
                c = c - tot
            acc[...] += upd
            carry[...] = c

        @pl.when(p == 0)
        def _():
            qbd[...] = jnp.where(diag, q_ref[...] * SB_SCALE, 0.0).astype(BF16)
            acc[...] = jnp.zeros_like(acc)
            carry[...] = jnp.zeros_like(carry)
            key = lax.broadcasted_iota(jnp.int32, (nr, PAGE_SIZE), 1)
            tq = lax.broadcasted_iota(jnp.int32, (nr, PAGE_SIZE), 0) // SB_HEADS
            accumulate([page(kn_ref, vn_ref, (key < tq) & (key < n_new))])

        accumulate([page(kp_refs[g], vp_refs[g], None) for g in range(grp)])

        @pl.when(p == n_pages // grp - 1)
        def _():
            o_ref[...] = jnp.sum(jnp.where(diag, acc[...], 0.0).reshape(n_new, SB_HEADS, d), axis=1)

    seq = lambda *shape: pl.BlockSpec((None,) + shape, lambda bi, p, pt: (bi, 0, 0))
    pagespec = lambda g: pl.BlockSpec((None, d, PAGE_SIZE),
                                      lambda bi, p, pt: (pt[bi, n_pages - 1 - (p * grp + g)], 0, 0))
    const = lambda c: pl.BlockSpec(c.shape, lambda bi, p, pt: (0, 0))
    pages = [pagespec(g) for g in range(grp)]
    out = pl.pallas_call(
        kern,
        grid_spec=pltpu.PrefetchScalarGridSpec(
            num_scalar_prefetch=1, grid=(bsz, n_pages // grp),
            in_specs=[seq(nr, d), seq(d, PAGE_SIZE), seq(d, PAGE_SIZE)] + pages + pages
            + [const(bias_col), const(u2)],
            out_specs=seq(n_new, d),
            scratch_shapes=[pltpu.VMEM((nr, d), BF16), pltpu.VMEM((nr, d), F32), pltpu.VMEM((nr, 1), F32)]),
        out_shape=jax.ShapeDtypeStruct((bsz, n_new, d), F32),
        compiler_params=pltpu.CompilerParams(dimension_semantics=("parallel", "arbitrary"),
                                             vmem_limit_bytes=VMEM_LIMIT),
        name="sb_sample")(page_table, q_rep, as_page(k_new), as_page(v_new), *([cache_kt] * grp),
                          *([cache_vt] * grp), bias_col, u2)
    return jnp.pad(out, ((0, 0), (0, rows - n_new), (0, 0)))


def _ret_tables(c, valid):
    log_gamma = jnp.log(1.0 - jnp.power(2.0, -5.0 - jnp.arange(RET_HEADS, dtype=F32)))
    idx = jnp.arange(c, dtype=F32)
    diff = idx[:, None] - idx[None, :]
    intra = jnp.where(diff >= 0, jnp.exp(jnp.maximum(diff, 0.0)[None] * log_gamma[:, None, None]), 0.0)
    q_decay = jnp.exp((idx + 1.0)[None, :] * log_gamma[:, None])[..., None]
    k_decay = jnp.where(idx < valid, jnp.exp((valid - 1.0 - idx)[None, :] * log_gamma[:, None]), 0.0)[..., None]
    chunk_decay = jnp.exp(valid * log_gamma)
    return intra, q_decay, k_decay, chunk_decay


def _retention_call(q, k, v, g, s0, gn_g, c, valid):
    b, t, _ = q.shape
    nc = t // c
    intra, q_decay, k_decay, chunk_decay = _ret_tables(c, valid)

    def kern(cd_ref, q_ref, k_ref, v_ref, g_ref, s0_ref, gn_ref, in_ref, qd_ref, kd_ref, y_ref, s_ref):
        h, ci = pl.program_id(1), pl.program_id(2)

        @pl.when(ci == 0)
        def _():
            s_ref[...] = s0_ref[...]

        state = s_ref[...]
        qc, kc = q_ref[...].astype(BF16), k_ref[...]
        vc = v_ref[...].astype(BF16)
        scores = lax.dot_general(qc, kc.astype(BF16), _NT, preferred_element_type=F32) * in_ref[...]
        o = (_dot(scores, vc) + _dot(qc, state.astype(BF16)) * qd_ref[...])
        s_ref[...] = state * cd_ref[h] + _dot((kc * kd_ref[...]).T, vc)
        y_ref[...] = _silu(g_ref[...]) * _layernorm(o, gn_ref[...])

    qk = pl.BlockSpec((None, c, RET_DK), lambda bi, h, ci: (bi, ci, h))
    vg = pl.BlockSpec((None, c, RET_DV), lambda bi, h, ci: (bi, ci, h))
    st = pl.BlockSpec((None, None, RET_DK, RET_DV), lambda bi, h, ci: (bi, h, 0, 0))
    per_head = lambda *shape: pl.BlockSpec((None,) + shape, lambda bi, h, ci: (h,) + (0,) * len(shape))
    return pl.pallas_call(
        kern, grid=(b, RET_HEADS, nc),
        in_specs=[pl.BlockSpec(memory_space=pltpu.SMEM), qk, qk, vg, vg, st, per_head(1, RET_DV),
                  per_head(c, c), per_head(c, 1), per_head(c, 1)],
        out_specs=[vg, st],
        out_shape=[jax.ShapeDtypeStruct((b, t, RET_HEADS * RET_DV), F32),
                   jax.ShapeDtypeStruct((b, RET_HEADS, RET_DK, RET_DV), F32)],
        compiler_params=pltpu.CompilerParams(dimension_semantics=("parallel", "parallel", "arbitrary"),
                                             vmem_limit_bytes=VMEM_LIMIT),
        name="retention")(chunk_decay, q, k, v, g, s0, gn_g, intra, q_decay, k_decay)


def _mem_attend_sample_call(q, cache_k, cache_v, layer):
    bsz, rows, d = q.shape
    m = cache_k.shape[2]

    def kern(q_ref, k_ref, v_ref, o_ref):
        qv = q_ref[...]
        heads = []
        for hd in range(MEM_HEADS):
            sl = slice(hd * MEM_HEAD_DIM, (hd + 1) * MEM_HEAD_DIM)
            heads.append(_mem_attend(qv[:, sl], k_ref[:, sl].astype(BF16), v_ref[:, sl].astype(BF16)))
        o_ref[...] = jnp.concatenate(heads, axis=-1)

    seq = pl.BlockSpec((None, rows, d), lambda bi: (bi, 0, 0))
    kv = pl.BlockSpec((None, None, m, d), lambda bi: (layer, bi, 0, 0))
    return pl.pallas_call(
        kern, grid=(bsz,), in_specs=[seq, kv, kv], out_specs=seq,
        out_shape=jax.ShapeDtypeStruct((bsz, rows, d), F32),
        compiler_params=pltpu.CompilerParams(dimension_semantics=("parallel",), vmem_limit_bytes=VMEM_LIMIT),
        name="mem_attend_sample")(q, cache_k, cache_v)


def _rope_tables(pos):
    half = RET_DK // 2
    inv_freq = jnp.power(ROPE_BASE, -jnp.arange(0, RET_DK, 2, dtype=F32) / RET_DK)
    ang = pos.astype(F32)[:, None] * inv_freq[None, :]
    return jnp.cos(ang), jnp.sin(ang)


def _trunk(x, t_valid, pos0, p, states, sb_past, mem):
    b, t, d = x.shape
    m = b * t
    tm = _pick_tile(t, ROW_TILE)
    tpb = t // tm
    if mem[0] == 'sample':
        tm, tpb = m, 1
    state_a, state_b, state_r = states
    row = lambda r: r.reshape(m, -1)
    vec = lambda g: g.reshape(1, -1)
    xf = row(x)
    ct = _pick_tile(t, CONV_TILE)
    new = {}
    for i in range(4):
        g_mix = vec(p['norm_mix'][i])
        if i == 0:
            (gl,) = _row_call("pre_conv_a", _pre_conv_a, m, tm, [xf], [(d, F32)],
                              consts=[g_mix, p['cva_w1'], vec(p['cva_b1'])])
            mix, new['a'] = _conv_call("conv_a", gl.reshape(b, t, d), state_a, p['cva_wdw'], _conv_a_epilogue,
                                       ct, t_valid if t_valid < t else ct,
                                       consts=[vec(p['cva_bdw']), vec(p['cva_ln_g']), vec(p['cva_ln_b'])])
            wo = p['cva_w2']
        elif i == 1:
            bg, cu = _row_call("pre_conv_b", _pre_conv_b, m, tm, [xf], [(d, F32)] * 2, consts=[g_mix, p['scb_win']])
            mix, new['b'] = _conv_call("conv_b", cu.reshape(b, t, d), state_b, p['scb_wconv'], _conv_b_epilogue,
                                       ct, t_valid if t_valid < t else ct,
                                       extra_rows=[bg.reshape(b, t, d)])
            wo = p['scb_wout']
        elif i == 2:
            if sb_past is None:
                w = p['sb_wqkv']
                qb, kt, vt, ktb, vtb = _row_call(
                    "pre_sb", _pre_sb_prompt, m, tm, [xf], [(d, BF16)], tiles_per_batch=tpb,
                    consts=[g_mix, w[:, :d], w[:, d:2 * d].T, w[:, 2 * d:].T], outs_t=[(d, F32)] * 2 + [(d, BF16)] * 2)
                mix = _sb_prompt_call(qb.reshape(b, t, d), ktb, vtb, p['sb_bias'], _pick_tile(t, SB_BLOCK))
                token_major = lambda a: jnp.transpose(a.reshape(b, SB_HEADS, SB_HEAD_DIM, t), (0, 3, 1, 2))
                new['k'], new['v'] = token_major(kt), token_major(vt)
            else:
                q, k, v = _row_call("pre_sb", _pre_sb, m, tm, [xf], [(d, F32)] * 3, consts=[g_mix, p['sb_wqkv']])
                cache_k, cache_v, page_table = sb_past
                mix = _sb_sample_call(q.reshape(b, t, d), k.reshape(b, t, d), v.reshape(b, t, d), cache_k, cache_v,
                                      page_table, p['sb_bias'], t_valid)
                per_head = lambda a: a.reshape(b, t, SB_HEADS, SB_HEAD_DIM)[:, :t_valid]
                new['k'], new['v'] = per_head(k), per_head(v)
            wo = p['sb_wo']
        else:
            cos, sin = _rope_tables(pos0 + jnp.arange(t))
            if mem[0] == 'sample':
                cos, sin = jnp.tile(cos, (b, 1)), jnp.tile(sin, (b, 1))
            q, k, v, g = _row_call("pre_ret", _pre_ret, m, tm, [xf], [(d, F32)] * 2 + [(2 * d, F32)] * 2,
                                   consts=[g_mix, p['ret_wqkvg']], prows=[cos, sin], tiles_per_batch=tpb)
            c = _pick_tile(t, RET_CHUNK)
            mix, new['r'] = _retention_call(q.reshape(b, t, d), k.reshape(b, t, d), v.reshape(b, t, 2 * d),
                                            g.reshape(b, t, 2 * d), state_r, p['ret_gn_g'].reshape(RET_HEADS, 1, RET_DV),
                                            c, min(c, t_valid))
            wo = p['ret_wo']
        mix = row(mix)
        g_cross, qn = vec(p['norm_cross'][i]), vec(p['xa_qnorm'][i])
        mlp_consts = [vec(p['norm_mlp'][i]), p['mlp_w1'][i], p['mlp_w2'][i]]
        if mem[0] == 'prompt':
            (x2,) = _row_call("post_cross", _post_cross_prompt, m, tm, [xf, mix], [(d, F32)],
                              consts=[wo, g_cross, p['xa_wq'][i], qn, p['xa_wo'][i]],
                              bconsts=[mem[1][i], mem[2][i]], tiles_per_batch=tpb)
            (xf,) = _row_call("post_mlp", _post_mlp, m, tm, [x2], [(d, F32)], consts=mlp_consts)
        else:
            x1, qm = _row_call("post_q", _post_q_sample, m, tm, [xf, mix], [(d, F32)] * 2,
                               consts=[wo, g_cross, p['xa_wq'][i], qn])
            o = _mem_attend_sample_call(qm.reshape(b, t, d), mem[1], mem[2], i)
            (xf,) = _row_call("post_xo_mlp", _post_xo_mlp, m, tm, [x1, row(o)], [(d, F32)],
                              consts=[p['xa_wo'][i]] + mlp_consts)
    return xf.reshape(b, t, d), new


def kernel(x_prompt, x_sample, mem_prompt, state_conv_a, state_conv_b, cache_sb_k, cache_sb_v, page_table, state_ret, cache_mem_k, cache_mem_v, norm_mix, norm_cross, norm_mem, norm_mlp, xa_wq, xa_wk, xa_wv, xa_wo, xa_qnorm, xa_knorm, mlp_w1, mlp_w2, cva_w1, cva_b1, cva_wdw, cva_bdw, cva_ln_g, cva_ln_b, cva_w2, scb_win, scb_wconv, scb_wout, sb_wqkv, sb_wo, sb_bias, ret_wqkvg, ret_gn_g, ret_wo):
    bp, seq, d = x_prompt.shape
    bs, dec_seq, _ = x_sample.shape
    depth = norm_mix.shape[0]
    assert depth == 4 and d == D_MODEL and dec_seq <= SAMPLE_ROWS
    n_mem = mem_prompt.shape[1]
    past_len = page_table.shape[1] * PAGE_SIZE
    wb = lambda w: w.astype(BF16)
    p = {
        'norm_mix': norm_mix, 'norm_cross': norm_cross, 'norm_mlp': norm_mlp,
        'xa_wq': wb(xa_wq), 'xa_wo': wb(xa_wo), 'xa_qnorm': xa_qnorm, 'mlp_w1': wb(mlp_w1), 'mlp_w2': wb(mlp_w2),
        'cva_w1': wb(cva_w1[0]), 'cva_b1': cva_b1[0], 'cva_wdw': cva_wdw[0], 'cva_bdw': cva_bdw[0],
        'cva_ln_g': cva_ln_g[0], 'cva_ln_b': cva_ln_b[0], 'cva_w2': wb(cva_w2[0]),
        'scb_win': wb(scb_win[0]), 'scb_wconv': scb_wconv[0], 'scb_wout': wb(scb_wout[0]),
        'sb_wqkv': wb(sb_wqkv[0]), 'sb_wo': wb(sb_wo[0]), 'sb_bias': sb_bias[0],
        'ret_wqkvg': wb(ret_wqkvg[0]), 'ret_gn_g': ret_gn_g[0], 'ret_wo': wb(ret_wo[0]),
    }

    mk, mv, mkb, mvb = _mem_kv_call(mem_prompt.reshape(bp * n_mem, d), norm_mem.reshape(depth, 1, d), wb(xa_wk),
                                    wb(xa_wv), xa_knorm.reshape(depth, 1, MEM_HEAD_DIM))
    per_seq = lambda a: a.reshape(depth, bp, n_mem, d)
    zeros = lambda *s: jnp.zeros(s, F32)
    y_p, new_p = _trunk(x_prompt, seq, 0, p,
                        (zeros(bp, CONV_A_WIDTH - 1, d), zeros(bp, CONV_B_WIDTH - 1, d),
                         zeros(bp, RET_HEADS, RET_DK, RET_DV)),
                        None, ('prompt', per_seq(mkb), per_seq(mvb)))

    xs = jnp.pad(x_sample, ((0, 0), (0, SAMPLE_ROWS - dec_seq), (0, 0)))
    pool = cache_sb_k.shape[1]
    feature_major = lambda c: jnp.transpose(c[0], (0, 2, 3, 1)).reshape(pool, d, PAGE_SIZE)
    y_s, new_s = _trunk(xs, dec_seq, past_len, p, (state_conv_a[0], state_conv_b[0], state_ret[0]),
                        (feature_major(cache_sb_k), feature_major(cache_sb_v),
                         page_table),
                        ('sample', cache_mem_k.reshape(depth, bs, n_mem, d), cache_mem_v.reshape(depth, bs, n_mem, d)))

    mem_out = lambda a: a.reshape(depth, bp, n_mem, MEM_HEADS, MEM_HEAD_DIM)
    return (y_p, y_s[:, :dec_seq], new_p['a'][None], new_s['a'][None], new_p['b'][None], new_s['b'][None],
            new_p['k'][None], new_p['v'][None], new_s['k'][None], new_s['v'][None],
            new_p['r'][None], new_s['r'][None], mem_out(mk), mem_out(mv))
```

```python
import functools
import math

import jax
import jax.numpy as jnp
from jax import lax
from jax.experimental import pallas as pl
from jax.experimental.pallas import tpu as pltpu

F32, BF16 = jnp.float32, jnp.bfloat16

D_MODEL = 1024
NORM_EPS = 1e-6
PAGE_SIZE = 128
CONV_A_WIDTH, CONV_B_WIDTH = 31, 3
SB_HEADS, SB_HEAD_DIM = 16, 64
SB_SCALE = SB_HEAD_DIM ** -0.5
RET_HEADS, RET_DK, RET_DV = 4, 256, 512
ROPE_BASE = 10000.0
MEM_HEADS, MEM_HEAD_DIM = 4, 256
MEM_SCALE = MEM_HEAD_DIM ** -0.5
D_FF = 4 * D_MODEL

CONV_HALO = 32
SAMPLE_ROWS = 8
VMEM_LIMIT = 56 * 1024 * 1024
ROW_TILE = 512
CONV_TILE = 256
SB_BLOCK = 256
SB_HEADS_PER_STEP = 4
SB_MASKED_LOGIT = -1e30
SB_PAGES_PER_STEP = 8
RET_CHUNK = 256

_NT = (((1,), (1,)), ((), ()))


def _dot(a, b):
    return jnp.dot(a.astype(BF16), b, preferred_element_type=F32)


def _dot_nt(a, b):
    return lax.dot_general(a.astype(BF16), b, _NT, preferred_element_type=F32)


def _rms(x, g):
    return x * lax.rsqrt(jnp.mean(x * x, axis=-1, keepdims=True) + NORM_EPS) * g


def _layernorm(x, g, b=None):
    xc = x - jnp.mean(x, axis=-1, keepdims=True)
    y = xc * lax.rsqrt(jnp.mean(xc * xc, axis=-1, keepdims=True) + NORM_EPS) * g
    return y if b is None else y + b


def _silu(x):
    return x * jax.nn.sigmoid(x)


def _pick_tile(n, pref):
    t = min(n, pref)
    while n % t:
        t -= 8
    return t


def _row_call(name, body, m, tm, rows, outs, consts=(), bconsts=(), prows=(), tiles_per_batch=1, outs_t=()):
    in_specs, args = [], []
    for r in rows:
        arr, width, cb = r if isinstance(r, tuple) else (r, r.shape[1], 0)
        in_specs.append(pl.BlockSpec((tm, width), lambda i, cb=cb: (i, cb)))
        args.append(arr)
    for p in prows:
        in_specs.append(pl.BlockSpec((tm, p.shape[1]), lambda i: (i % tiles_per_batch, 0)))
        args.append(p)
    for c in bconsts:
        in_specs.append(pl.BlockSpec((None,) + c.shape[1:],
                                     lambda i, nd=c.ndim: (i // tiles_per_batch,) + (0,) * (nd - 1)))
        args.append(c)
    for c in consts:
        in_specs.append(pl.BlockSpec(c.shape, lambda i, nd=c.ndim: (0,) * nd, pipeline_mode=pl.Buffered(1)))
        args.append(c)
    n_r, n_p, n_b, n_c = len(rows), len(prows), len(bconsts), len(consts)

    def kern(*refs):
        a, b, c, d = n_r, n_r + n_p, n_r + n_p + n_b, n_r + n_p + n_b + n_c
        body(refs[:a], refs[a:b], refs[b:c], refs[c:d], refs[d:])

    t = tm * tiles_per_batch
    fm_spec = lambda f: pl.BlockSpec((None, f, tm), lambda i: (i // tiles_per_batch, 0, i % tiles_per_batch))
    return pl.pallas_call(
        kern, grid=(m // tm,), in_specs=in_specs,
        out_specs=[pl.BlockSpec((tm, w), lambda i: (i, 0)) for w, _ in outs] + [fm_spec(f) for f, _ in outs_t],
        out_shape=[jax.ShapeDtypeStruct((m, w), dt) for w, dt in outs]
        + [jax.ShapeDtypeStruct((m // t, f, t), dt) for f, dt in outs_t],
        compiler_params=pltpu.CompilerParams(dimension_semantics=("parallel",), vmem_limit_bytes=VMEM_LIMIT),
        name=name)(*args)


def _mem_q(x1, g_cross, wq_ref, qn):
    q = _dot(_rms(x1, g_cross), wq_ref[...])
    heads = []
    for hd in range(MEM_HEADS):
        qh = q[:, hd * MEM_HEAD_DIM:(hd + 1) * MEM_HEAD_DIM]
        heads.append(qh * lax.rsqrt(jnp.mean(qh * qh, axis=-1, keepdims=True) + NORM_EPS) * qn)
    return heads


def _mem_attend(qh, mk_h, mv_h):
    s = _dot_nt(qh, mk_h) * MEM_SCALE
    p = jnp.exp(s - jnp.max(s, axis=-1, keepdims=True))
    a = p / jnp.sum(p, axis=-1, keepdims=True)
    return _dot(a, mv_h)


def _mlp(x2, g_mlp, w1_ref, w2_ref):
    h = _rms(x2, g_mlp).astype(BF16)
    acc = x2
    for c in range(D_FF // D_MODEL):
        sl = slice(c * D_MODEL, (c + 1) * D_MODEL)
        u = jnp.dot(h, w1_ref[:, sl], preferred_element_type=F32)
        u = jnp.square(jnp.maximum(u, 0.0))
        acc = acc + _dot(u, w2_ref[sl, :])
    return acc


def _rotary(x, cos, sin):
    half = RET_DK // 2
    parts = []
    for hd in range(RET_HEADS):
        x1 = x[:, hd * RET_DK: hd * RET_DK + half]
        x2 = x[:, hd * RET_DK + half: (hd + 1) * RET_DK]
        parts += [x1 * cos - x2 * sin, x1 * sin + x2 * cos]
    return jnp.concatenate(parts, axis=-1)


def _pre_conv_a(rows, prows, bcs, cs, outs):
    g_mix, w1, b1 = cs
    h = _rms(rows[0][...], g_mix[...])
    a = _dot(h, w1[:, :D_MODEL]) + b1[:, :D_MODEL]
    gate = _dot(h, w1[:, D_MODEL:]) + b1[:, D_MODEL:]
    outs[0][...] = a * jax.nn.sigmoid(gate)


def _pre_conv_b(rows, prows, bcs, cs, outs):
    g_mix, win = cs
    h = _rms(rows[0][...], g_mix[...])
    outs[0][...] = _dot(h, win[:, :D_MODEL])
    outs[1][...] = _dot(h, win[:, D_MODEL:2 * D_MODEL]) * _dot(h, win[:, 2 * D_MODEL:])


def _pre_sb(rows, prows, bcs, cs, outs):
    g_mix, wqkv = cs
    h = _rms(rows[0][...], g_mix[...])
    q = _dot(h, wqkv[:, :D_MODEL])
    k = _dot(h, wqkv[:, D_MODEL:2 * D_MODEL])
    v = _dot(h, wqkv[:, 2 * D_MODEL:])
    outs[0][...] = q
    outs[1][...] = k
    outs[2][...] = v


def _pre_sb_prompt(rows, prows, bcs, cs, outs):
    g_mix, wq, wkt, wvt = cs
    h = _rms(rows[0][...], g_mix[...]).astype(BF16)
    outs[0][...] = (jnp.dot(h, wq[...], preferred_element_type=F32) * SB_SCALE).astype(BF16)
    kt = lax.dot_general(wkt[...], h, _NT, preferred_element_type=F32)
    vt = lax.dot_general(wvt[...], h, _NT, preferred_element_type=F32)
    outs[1][...] = kt
    outs[2][...] = vt
    outs[3][...] = kt.astype(BF16)
    outs[4][...] = vt.astype(BF16)


def _pre_ret(rows, prows, bcs, cs, outs):
    g_mix, w = cs
    cos, sin = prows[0][...], prows[1][...]
    h = _rms(rows[0][...], g_mix[...])
    d = D_MODEL
    outs[0][...] = _rotary(_dot(h, w[:, :d]), cos, sin)
    outs[1][...] = _rotary(_dot(h, w[:, d:2 * d]), cos, sin) * (RET_DK ** -0.5)
    outs[2][...] = _dot(h, w[:, 2 * d:4 * d])
    outs[3][...] = _dot(h, w[:, 4 * d:])


def _post_cross_prompt(rows, prows, bcs, cs, outs):
    x_ref, mix_ref = rows
    mk_ref, mv_ref = bcs
    wo, g_cross, wq, qn, xwo = cs
    x1 = x_ref[...] + _dot(mix_ref[...], wo[...])
    acc = x1
    for hd, qh in enumerate(_mem_q(x1, g_cross[...], wq, qn[...])):
        sl = slice(hd * MEM_HEAD_DIM, (hd + 1) * MEM_HEAD_DIM)
        acc = acc + _dot(_mem_attend(qh, mk_ref[:, sl], mv_ref[:, sl]), xwo[sl, :])
    outs[0][...] = acc


def _post_q_sample(rows, prows, bcs, cs, outs):
    x_ref, mix_ref = rows
    wo, g_cross, wq, qn = cs
    x1 = x_ref[...] + _dot(mix_ref[...], wo[...])
    outs[0][...] = x1
    outs[1][...] = jnp.concatenate(_mem_q(x1, g_cross[...], wq, qn[...]), axis=-1)


def _post_mlp(rows, prows, bcs, cs, outs):
    g_mlp, w1, w2 = cs
    outs[0][...] = _mlp(rows[0][...], g_mlp[...], w1, w2)


def _post_xo_mlp(rows, prows, bcs, cs, outs):
    xwo, g_mlp, w1, w2 = cs
    x2 = rows[0][...] + _dot(rows[1][...], xwo[...])
    outs[0][...] = _mlp(x2, g_mlp[...], w1, w2)


def _mem_kv_call(mem, norm_mem, wk, wv, knorm):
    depth = wk.shape[0]
    m = mem.shape[0]

    def kern(mem_ref, g_ref, wk_ref, wv_ref, kn_ref, k_ref, v_ref, kb_ref, vb_ref):
        mn = _rms(mem_ref[...], g_ref[...])
        k = _dot(mn, wk_ref[...])
        kn = kn_ref[...]
        heads = []
        for hd in range(MEM_HEADS):
            kh = k[:, hd * MEM_HEAD_DIM:(hd + 1) * MEM_HEAD_DIM]
            heads.append(kh * lax.rsqrt(jnp.mean(kh * kh, axis=-1, keepdims=True) + NORM_EPS) * kn)
        k = jnp.concatenate(heads, axis=-1)
        v = _dot(mn, wv_ref[...])
        k_ref[...] = k
        v_ref[...] = v
        kb_ref[...] = k.astype(BF16)
        vb_ref[...] = v.astype(BF16)

    lay = lambda *shape: pl.BlockSpec((None,) + shape, lambda i: (i,) + (0,) * len(shape))
    return pl.pallas_call(
        kern, grid=(depth,),
        in_specs=[pl.BlockSpec(mem.shape, lambda i: (0, 0)), lay(1, D_MODEL), lay(D_MODEL, D_MODEL),
                  lay(D_MODEL, D_MODEL), lay(1, MEM_HEAD_DIM)],
        out_specs=[lay(m, D_MODEL)] * 4,
        out_shape=[jax.ShapeDtypeStruct((depth, m, D_MODEL), F32)] * 2
        + [jax.ShapeDtypeStruct((depth, m, D_MODEL), BF16)] * 2,
        compiler_params=pltpu.CompilerParams(dimension_semantics=("parallel",), vmem_limit_bytes=VMEM_LIMIT),
        name="mem_kv")(mem, norm_mem, wk, wv, knorm)


def _conv_call(name, xin, state, w, epilogue, tm, valid_last, extra_rows=(), consts=()):
    b, t, d = xin.shape
    ktaps = w.shape[0]
    off = CONV_HALO - (ktaps - 1)
    nt = t // tm
    rb = min(16, tm)
    state_pad = jnp.pad(state, ((0, 0), (off, 0), (0, 0)))
    n_e, n_c = len(extra_rows), len(consts)
    shifts = sorted({(off + k) % 8 for k in range(ktaps)} - {0})
    span = tm + CONV_HALO - 8

    def kern(x_ref, st_ref, w_ref, *rest):
        extra, cs = rest[:n_e], rest[n_e:n_e + n_c]
        o_ref, ns_ref, xpad, xs = rest[n_e + n_c:]
        ti = pl.program_id(1)

        @pl.when(ti == 0)
        def _():
            xpad[0:CONV_HALO, :] = st_ref[...]

        xpad[CONV_HALO:CONV_HALO + tm, :] = x_ref[...]
        for si, r in enumerate(shifts):
            xs[si, 0:span, :] = xpad[r:r + span, :]

        def chunk(c, carry):
            r0 = pl.multiple_of(c * rb, rb)
            acc = jnp.zeros((rb, d), F32)
            for k in range(ktaps):
                base, r = (off + k) // 8 * 8, (off + k) % 8
                rows = pl.ds(pl.multiple_of(r0 + base, 8), rb)
                src = xpad[rows, :] if r == 0 else xs[shifts.index(r), rows, :]
                acc = acc + w_ref[k:k + 1, :] * src
            o_ref[pl.ds(r0, rb), :] = epilogue(acc, [e[pl.ds(r0, rb), :] for e in extra], cs)
            return carry

        lax.fori_loop(0, tm // rb, chunk, 0)

        @pl.when(ti == nt - 1)
        def _():
            ns_ref[...] = xpad[valid_last:valid_last + CONV_HALO, :]

        @pl.when(ti < nt - 1)
        def _():
            xpad[0:CONV_HALO, :] = xpad[tm:tm + CONV_HALO, :]

    tile = pl.BlockSpec((None, tm, d), lambda bi, ti: (bi, ti, 0))
    per_seq = pl.BlockSpec((None, CONV_HALO, d), lambda bi, ti: (bi, 0, 0))
    const = lambda c: pl.BlockSpec(c.shape, lambda bi, ti, nd=c.ndim: (0,) * nd)
    out, ns = pl.pallas_call(
        kern, grid=(b, nt),
        in_specs=[tile, per_seq, const(w)] + [tile] * n_e + [const(c) for c in consts],
        out_specs=[tile, per_seq],
        out_shape=[jax.ShapeDtypeStruct((b, t, d), F32), jax.ShapeDtypeStruct((b, CONV_HALO, d), F32)],
        scratch_shapes=[pltpu.VMEM((CONV_HALO + tm, d), F32), pltpu.VMEM((len(shifts), CONV_HALO + tm, d), F32)],
        compiler_params=pltpu.CompilerParams(dimension_semantics=("parallel", "arbitrary"),
                                             vmem_limit_bytes=VMEM_LIMIT),
        name=name)(xin, state_pad, w, *extra_rows, *consts)
    return out, ns[:, off:, :]


def _conv_a_epilogue(acc, extra, cs):
    bdw, ln_g, ln_b = cs
    return _silu(_layernorm(acc + bdw[...], ln_g[...], ln_b[...]))


def _conv_b_epilogue(acc, extra, cs):
    return extra[0] * acc


def _neg_strict_lower_x2(n):
    m = -(jnp.arange(n)[:, None] > jnp.arange(n)[None, :]).astype(BF16)
    return jnp.concatenate([m, m], axis=0)


def _sb_block_weights(z, u2, mask):
    nlk = jnp.maximum(z, 0.0) + jnp.log(1.0 + jnp.exp(-jnp.abs(z)))
    if mask is not None:
        nlk = jnp.where(mask, nlk, 0.0)
    hi = nlk.astype(BF16)
    lo = (nlk - hi.astype(F32)).astype(BF16)
    suffix = jnp.dot(jnp.concatenate([hi, lo], axis=1), u2, preferred_element_type=F32)
    a = jnp.exp(z - nlk + suffix)
    if mask is not None:
        a = jnp.where(mask, a, 0.0)
    return a, jnp.sum(nlk, axis=-1, keepdims=True)


def _sb_prompt_call(qb, kb, vb, bias, blk):
    b, t, d = qb.shape
    nq = t // blk
    hps = SB_HEADS_PER_STEP
    hw = hps * SB_HEAD_DIM
    u2 = _neg_strict_lower_x2(blk)[:blk]

    def kern(bias_ref, q_ref, k_ref, v_ref, u_ref, o_ref, acc_ref, carry_ref, hl_ref, d_ref, a_ref):
        hg, qi = pl.program_id(1), pl.program_id(2)
        tri = lax.broadcasted_iota(jnp.int32, (blk, blk), 1) < lax.broadcasted_iota(jnp.int32, (blk, blk), 0)
        lanes = lambda hh: slice(hh * SB_HEAD_DIM, (hh + 1) * SB_HEAD_DIM)
        cols = lambda kj: pl.ds(pl.multiple_of(kj * blk, blk), blk)

        def logits(hh, kj, mask):
            z = jnp.dot(q_ref[:, lanes(hh)], k_ref[lanes(hh), cols(kj)], preferred_element_type=F32)
            z = z + bias_ref[hg * hps + hh]
            neg_abs = lax.bitcast_convert_type(lax.bitcast_convert_type(z, jnp.uint32) | jnp.uint32(1 << 31), F32)
            nlk = jnp.maximum(z, 0.0) + jnp.log(1.0 + jnp.exp(neg_abs))
            if mask is not None:
                nlk = jnp.where(mask, nlk, 0.0)
            hl_ref[hh] = nlk.astype(BF16)
            c = carry_ref[hh]
            dd = z - nlk + c
            d_ref[hh] = dd if mask is None else jnp.where(mask, dd, SB_MASKED_LOGIT)
            carry_ref[hh] = c - jnp.sum(nlk, axis=-1, keepdims=True)

        def weights(hh):
            suffix = jnp.dot(hl_ref[hh], u_ref[...], preferred_element_type=F32)
            a_ref[hh] = jnp.exp(d_ref[hh] + suffix).astype(BF16)

        def values(hh, kj):
            acc_ref[hh] += lax.dot_general(a_ref[hh], v_ref[lanes(hh), cols(kj)], _NT, preferred_element_type=F32)

        acc_ref[...] = jnp.zeros_like(acc_ref)
        carry_ref[...] = jnp.zeros_like(carry_ref)
        a_ref[...] = jnp.zeros_like(a_ref)
        for hh in range(hps):
            logits(hh, qi, tri)

        def older(it, carry):
            kj = qi - 1 - it
            for hh in range(hps):
                values(hh, jnp.minimum(kj + 2, nq - 1))
            for hh in range(hps):
                weights(hh)
            for hh in range(hps):
                logits(hh, kj, None)
            return carry

        lax.fori_loop(0, qi, older, 0)
        for hh in range(hps):
            values(hh, min(1, nq - 1))
        for hh in range(hps):
            weights(hh)
        for hh in range(hps):
            values(hh, 0)
        o_ref[...] = jnp.concatenate([acc_ref[hh] for hh in range(hps)], axis=-1)

    tile = pl.BlockSpec((None, blk, hw), lambda bi, hg, qi: (bi, qi, hg))
    seq = pl.BlockSpec((None, hw, t), lambda bi, hg, qi: (bi, hg, 0))
    return pl.pallas_call(
        kern, grid=(b, SB_HEADS // hps, nq),
        in_specs=[pl.BlockSpec(memory_space=pltpu.SMEM), tile, seq, seq,
                  pl.BlockSpec(u2.shape, lambda bi, hg, qi: (0, 0))],
        out_specs=tile,
        out_shape=jax.ShapeDtypeStruct((b, t, d), F32),
        scratch_shapes=[pltpu.VMEM((hps, blk, SB_HEAD_DIM), F32), pltpu.VMEM((hps, blk, 1), F32),
                        pltpu.VMEM((hps, blk, blk), BF16), pltpu.VMEM((hps, blk, blk), F32),
                        pltpu.VMEM((hps, blk, blk), BF16)],
        compiler_params=pltpu.CompilerParams(dimension_semantics=("parallel", "parallel", "arbitrary"),
                                             vmem_limit_bytes=VMEM_LIMIT),
        name="sb_prompt")(bias, qb, kb, vb, u2)


def _sb_sample_call(q, k_new, v_new, cache_kt, cache_vt, page_table, bias, n_new):
    bsz, rows, d = q.shape
    n_pages = page_table.shape[1]
    grp = math.gcd(n_pages, SB_PAGES_PER_STEP)
    nr = n_new * SB_HEADS
    q_rep = jnp.repeat(q[:, :n_new], SB_HEADS, axis=1)
    bias_col = jnp.tile(bias, n_new)[:, None]
    u2 = _neg_strict_lower_x2(PAGE_SIZE)
    as_page = lambda x: jnp.swapaxes(jnp.pad(x, ((0, 0), (0, PAGE_SIZE - rows), (0, 0))), 1, 2)

    def kern(pt_ref, q_ref, kn_ref, vn_ref, *rest):
        kp_refs, vp_refs = rest[:grp], rest[grp:2 * grp]
        bias_ref, u_ref, o_ref, qbd, acc, carry = rest[2 * grp:]
        p = pl.program_id(1)
        row = lax.broadcasted_iota(jnp.int32, (nr, d), 0)
        lane = lax.broadcasted_iota(jnp.int32, (nr, d), 1)
        diag = (row % SB_HEADS) == (lane // SB_HEAD_DIM)

        def page(kt_ref, vt_ref, mask):
            z = jnp.dot(qbd[...], kt_ref[...].astype(BF16), preferred_element_type=F32) + bias_ref[...]
            a, tot = _sb_block_weights(z, u_ref[...], mask)
            return _dot_nt(a, vt_ref[...].astype(BF16)), tot

        def accumulate(parts):
            c = carry[...]
            upd = None
            for pv, tot in parts:
                term = jnp.exp(c) * pv
                upd = term if upd is None else upd + term
                c = c - tot
            acc[...] += upd
            carry[...] = c

        @pl.when(p == 0)
        def _():
            qbd[...] = jnp.where(diag, q_ref[...] * SB_SCALE, 0.0).astype(BF16)
            acc[...] = jnp.zeros_like(acc)
            carry[...] = jnp.zeros_like(carry)
            key = lax.broadcasted_iota(jnp.int32, (nr, PAGE_SIZE), 1)
            tq = lax.broadcasted_iota(jnp.int32, (nr, PAGE_SIZE), 0) // SB_HEADS
            accumulate([page(kn_ref, vn_ref, (key < tq) & (key < n_new))])

        accumulate([page(kp_refs[g], vp_refs[g], None) for g in range(grp)])

        @pl.when(p == n_pages // grp - 1)
        def _():
            o_ref[...] = jnp.sum(jnp.where(diag, acc[...], 0.0).reshape(n_new, SB_HEADS, d), axis=1)

    seq = lambda *shape: pl.BlockSpec((None,) + shape, lambda bi, p, pt: (bi, 0, 0))
    pagespec = lambda g: pl.BlockSpec((None, d, PAGE_SIZE),
                                      lambda bi, p, pt: (pt[bi, n_pages - 1 - (p * grp + g)], 0, 0))
    const = lambda c: pl.BlockSpec(c.shape, lambda bi, p, pt: (0, 0))
    pages = [pagespec(g) for g in range(grp)]
    out = pl.pallas_call(
        kern,
        grid_spec=pltpu.PrefetchScalarGridSpec(
            num_scalar_prefetch=1, grid=(bsz, n_pages // grp),
            in_specs=[seq(nr, d), seq(d, PAGE_SIZE), seq(d, PAGE_SIZE)] + pages + pages
            + [const(bias_col), const(u2)],
            out_specs=seq(n_new, d),
            scratch_shapes=[pltpu.VMEM((nr, d), BF16), pltpu.VMEM((nr, d), F32), pltpu.VMEM((nr, 1), F32)]),
        out_shape=jax.ShapeDtypeStruct((bsz, n_new, d), F32),
        compiler_params=pltpu.CompilerParams(dimension_semantics=("parallel", "arbitrary"),
                                             vmem_limit_bytes=VMEM_LIMIT),
        name="sb_sample")(page_table, q_rep, as_page(k_new), as_page(v_new), *([cache_kt] * grp),
                          *([cache_vt] * grp), bias_col, u2)
    return jnp.pad(out, ((0, 0), (0, rows - n_new), (0, 0)))


def _ret_tables(c, valid):
    log_gamma = jnp.log(1.0 - jnp.power(2.0, -5.0 - jnp.arange(RET_HEADS, dtype=F32)))
    idx = jnp.arange(c, dtype=F32)
    diff = idx[:, None] - idx[None, :]
    intra = jnp.where(diff >= 0, jnp.exp(jnp.maximum(diff, 0.0)[None] * log_gamma[:, None, None]), 0.0)
    q_decay = jnp.exp((idx + 1.0)[None, :] * log_gamma[:, None])[..., None]
    k_decay = jnp.where(idx < valid, jnp.exp((valid - 1.0 - idx)[None, :] * log_gamma[:, None]), 0.0)[..., None]
    chunk_decay = jnp.exp(valid * log_gamma)
    return intra, q_decay, k_decay, chunk_decay


def _retention_call(q, k, v, g, s0, gn_g, c, valid):
    b, t, _ = q.shape
    nc = t // c
    intra, q_decay, k_decay, chunk_decay = _ret_tables(c, valid)

    def kern(cd_ref, q_ref, k_ref, v_ref, g_ref, s0_ref, gn_ref, in_ref, qd_ref, kd_ref, y_ref, s_ref):
        h, ci = pl.program_id(1), pl.program_id(2)

        @pl.when(ci == 0)
        def _():
            s_ref[...] = s0_ref[...]

        state = s_ref[...]
        qc, kc = q_ref[...].astype(BF16), k_ref[...]
        vc = v_ref[...].astype(BF16)
        scores = lax.dot_general(qc, kc.astype(BF16), _NT, preferred_element_type=F32) * in_ref[...]
        o = (_dot(scores, vc) + _dot(qc, state.astype(BF16)) * qd_ref[...])
        s_ref[...] = state * cd_ref[h] + _dot((kc * kd_ref[...]).T, vc)
        y_ref[...] = _silu(g_ref[...]) * _layernorm(o, gn_ref[...])

    qk = pl.BlockSpec((None, c, RET_DK), lambda bi, h, ci: (bi, ci, h))
    vg = pl.BlockSpec((None, c, RET_DV), lambda bi, h, ci: (bi, ci, h))
    st = pl.BlockSpec((None, None, RET_DK, RET_DV), lambda bi, h, ci: (bi, h, 0, 0))
    per_head = lambda *shape: pl.BlockSpec((None,) + shape, lambda bi, h, ci: (h,) + (0,) * len(shape))
    return pl.pallas_call(
        kern, grid=(b, RET_HEADS, nc),
        in_specs=[pl.BlockSpec(memory_space=pltpu.SMEM), qk, qk, vg, vg, st, per_head(1, RET_DV),
                  per_head(c, c), per_head(c, 1), per_head(c, 1)],
        out_specs=[vg, st],
        out_shape=[jax.ShapeDtypeStruct((b, t, RET_HEADS * RET_DV), F32),
                   jax.ShapeDtypeStruct((b, RET_HEADS, RET_DK, RET_DV), F32)],
        compiler_params=pltpu.CompilerParams(dimension_semantics=("parallel", "parallel", "arbitrary"),
                                             vmem_limit_bytes=VMEM_LIMIT),
        name="retention")(chunk_decay, q, k, v, g, s0, gn_g, intra, q_decay, k_decay)


def _mem_attend_sample_call(q, cache_k, cache_v, layer):
    bsz, rows, d = q.shape
    m = cache_k.shape[2]

    def kern(q_ref, k_ref, v_ref, o_ref):
        qv = q_ref[...]
        heads = []
        for hd in range(MEM_HEADS):
            sl = slice(hd * MEM_HEAD_DIM, (hd + 1) * MEM_HEAD_DIM)
            heads.append(_mem_attend(qv[:, sl], k_ref[:, sl].astype(BF16), v_ref[:, sl].astype(BF16)))
        o_ref[...] = jnp.concatenate(heads, axis=-1)

    seq = pl.BlockSpec((None, rows, d), lambda bi: (bi, 0, 0))
    kv = pl.BlockSpec((None, None, m, d), lambda bi: (layer, bi, 0, 0))
    return pl.pallas_call(
        kern, grid=(bsz,), in_specs=[seq, kv, kv], out_specs=seq,
        out_shape=jax.ShapeDtypeStruct((bsz, rows, d), F32),
        compiler_params=pltpu.CompilerParams(dimension_semantics=("parallel",), vmem_limit_bytes=VMEM_LIMIT),
        name="mem_attend_sample")(q, cache_k, cache_v)


def _rope_tables(pos):
    half = RET_DK // 2
    inv_freq = jnp.power(ROPE_BASE, -jnp.arange(0, RET_DK, 2, dtype=F32) / RET_DK)
    ang = pos.astype(F32)[:, None] * inv_freq[None, :]
    return jnp.cos(ang), jnp.sin(ang)


def _trunk(x, t_valid, pos0, p, states, sb_past, mem):
    b, t, d = x.shape
    m = b * t
    tm = _pick_tile(t, ROW_TILE)
    tpb = t // tm
    if mem[0] == 'sample':
        tm, tpb = m, 1
    state_a, state_b, state_r = states
    row = lambda r: r.reshape(m, -1)
    vec = lambda g: g.reshape(1, -1)
    xf = row(x)
    ct = _pick_tile(t, CONV_TILE)
    new = {}
    for i in range(4):
        g_mix = vec(p['norm_mix'][i])
        if i == 0:
            (gl,) = _row_call("pre_conv_a", _pre_conv_a, m, tm, [xf], [(d, F32)],
                              consts=[g_mix, p['cva_w1'], vec(p['cva_b1'])])
            mix, new['a'] = _conv_call("conv_a", gl.reshape(b, t, d), state_a, p['cva_wdw'], _conv_a_epilogue,
                                       ct, t_valid if t_valid < t else ct,
                                       consts=[vec(p['cva_bdw']), vec(p['cva_ln_g']), vec(p['cva_ln_b'])])
            wo = p['cva_w2']
        elif i == 1:
            bg, cu = _row_call("pre_conv_b", _pre_conv_b, m, tm, [xf], [(d, F32)] * 2, consts=[g_mix, p['scb_win']])
            mix, new['b'] = _conv_call("conv_b", cu.reshape(b, t, d), state_b, p['scb_wconv'], _conv_b_epilogue,
                                       ct, t_valid if t_valid < t else ct,
                                       extra_rows=[bg.reshape(b, t, d)])
            wo = p['scb_wout']
        elif i == 2:
            if sb_past is None:
                w = p['sb_wqkv']
                qb, kt, vt, ktb, vtb = _row_call(
                    "pre_sb", _pre_sb_prompt, m, tm, [xf], [(d, BF16)], tiles_per_batch=tpb,
                    consts=[g_mix, w[:, :d], w[:, d:2 * d].T, w[:, 2 * d:].T], outs_t=[(d, F32)] * 2 + [(d, BF16)] * 2)
                mix = _sb_prompt_call(qb.reshape(b, t, d), ktb, vtb, p['sb_bias'], _pick_tile(t, SB_BLOCK))
                token_major = lambda a: jnp.transpose(a.reshape(b, SB_HEADS, SB_HEAD_DIM, t), (0, 3, 1, 2))
                new['k'], new['v'] = token_major(kt), token_major(vt)
            else:
                q, k, v = _row_call("pre_sb", _pre_sb, m, tm, [xf], [(d, F32)] * 3, consts=[g_mix, p['sb_wqkv']])
                cache_k, cache_v, page_table = sb_past
                mix = _sb_sample_call(q.reshape(b, t, d), k.reshape(b, t, d), v.reshape(b, t, d), cache_k, cache_v,
                                      page_table, p['sb_bias'], t_valid)
                per_head = lambda a: a.reshape(b, t, SB_HEADS, SB_HEAD_DIM)[:, :t_valid]
                new['k'], new['v'] = per_head(k), per_head(v)
            wo = p['sb_wo']
        else:
            cos, sin = _rope_tables(pos0 + jnp.arange(t))
            if mem[0] == 'sample':
                cos, sin = jnp.tile(cos, (b, 1)), jnp.tile(sin, (b, 1))
            q, k, v, g = _row_call("pre_ret", _pre_ret, m, tm, [xf], [(d, F32)] * 2 + [(2 * d, F32)] * 2,
                                   consts=[g_mix, p['ret_wqkvg']], prows=[cos, sin], tiles_per_batch=tpb)
            c = _pick_tile(t, RET_CHUNK)
            mix, new['r'] = _retention_call(q.reshape(b, t, d), k.reshape(b, t, d), v.reshape(b, t, 2 * d),
                                            g.reshape(b, t, 2 * d), state_r, p['ret_gn_g'].reshape(RET_HEADS, 1, RET_DV),
                                            c, min(c, t_valid))
            wo = p['ret_wo']
        mix = row(mix)
        g_cross, qn = vec(p['norm_cross'][i]), vec(p['xa_qnorm'][i])
        mlp_consts = [vec(p['norm_mlp'][i]), p['mlp_w1'][i], p['mlp_w2'][i]]
        if mem[0] == 'prompt':
            (x2,) = _row_call("post_cross", _post_cross_prompt, m, tm, [xf, mix], [(d, F32)],
                              consts=[wo, g_cross, p['xa_wq'][i], qn, p['xa_wo'][i]],
                              bconsts=[mem[1][i], mem[2][i]], tiles_per_batch=tpb)
            (xf,) = _row_call("post_mlp", _post_mlp, m, tm, [x2], [(d, F32)], consts=mlp_consts)
        else:
            x1, qm = _row_call("post_q", _post_q_sample, m, tm, [xf, mix], [(d, F32)] * 2,
                               consts=[wo, g_cross, p['xa_wq'][i], qn])
            o = _mem_attend_sample_call(qm.reshape(b, t, d), mem[1], mem[2], i)
            (xf,) = _row_call("post_xo_mlp", _post_xo_mlp, m, tm, [x1, row(o)], [(d, F32)],
                              consts=[p['xa_wo'][i]] + mlp_consts)
    return xf.reshape(b, t, d), new


def kernel(x_prompt, x_sample, mem_prompt, state_conv_a, state_conv_b, cache_sb_k, cache_sb_v, page_table, state_ret, cache_mem_k, cache_mem_v, norm_mix, norm_cross, norm_mem, norm_mlp, xa_wq, xa_wk, xa_wv, xa_wo, xa_qnorm, xa_knorm, mlp_w1, mlp_w2, cva_w1, cva_b1, cva_wdw, cva_bdw, cva_ln_g, cva_ln_b, cva_w2, scb_win, scb_wconv, scb_wout, sb_wqkv, sb_wo, sb_bias, ret_wqkvg, ret_gn_g, ret_wo):
    bp, seq, d = x_prompt.shape
    bs, dec_seq, _ = x_sample.shape
    depth = norm_mix.shape[0]
    assert depth == 4 and d == D_MODEL and dec_seq <= SAMPLE_ROWS
    n_mem = mem_prompt.shape[1]
    past_len = page_table.shape[1] * PAGE_SIZE
    wb = lambda w: w.astype(BF16)
    p = {
        'norm_mix': norm_mix, 'norm_cross': norm_cross, 'norm_mlp': norm_mlp,
        'xa_wq': wb(xa_wq), 'xa_wo': wb(xa_wo), 'xa_qnorm': xa_qnorm, 'mlp_w1': wb(mlp_w1), 'mlp_w2': wb(mlp_w2),
        'cva_w1': wb(cva_w1[0]), 'cva_b1': cva_b1[0], 'cva_wdw': cva_wdw[0], 'cva_bdw': cva_bdw[0],
        'cva_ln_g': cva_ln_g[0], 'cva_ln_b': cva_ln_b[0], 'cva_w2': wb(cva_w2[0]),
        'scb_win': wb(scb_win[0]), 'scb_wconv': scb_wconv[0], 'scb_wout': wb(scb_wout[0]),
        'sb_wqkv': wb(sb_wqkv[0]), 'sb_wo': wb(sb_wo[0]), 'sb_bias': sb_bias[0],
        'ret_wqkvg': wb(ret_wqkvg[0]), 'ret_gn_g': ret_gn_g[0], 'ret_wo': wb(ret_wo[0]),
    }

    mk, mv, mkb, mvb = _mem_kv_call(mem_prompt.reshape(bp * n_mem, d), norm_mem.reshape(depth, 1, d), wb(xa_wk),
                                    wb(xa_wv), xa_knorm.reshape(depth, 1, MEM_HEAD_DIM))
    per_seq = lambda a: a.reshape(depth, bp, n_mem, d)
    zeros = lambda *s: jnp.zeros(s, F32)
    y_p, new_p = _trunk(x_prompt, seq, 0, p,
                        (zeros(bp, CONV_A_WIDTH - 1, d), zeros(bp, CONV_B_WIDTH - 1, d),
                         zeros(bp, RET_HEADS, RET_DK, RET_DV)),
                        None, ('prompt', per_seq(mkb), per_seq(mvb)))

    xs = jnp.pad(x_sample, ((0, 0), (0, SAMPLE_ROWS - dec_seq), (0, 0)))
    pool = cache_sb_k.shape[1]
    feature_major = lambda c: jnp.transpose(c[0], (0, 2, 3, 1)).reshape(pool, d, PAGE_SIZE)
    y_s, new_s = _trunk(xs, dec_seq, past_len, p, (state_conv_a[0], state_conv_b[0], state_ret[0]),
                        (feature_major(cache_sb_k), feature_major(cache_sb_v),
                         page_table),
                        ('sample', cache_mem_k.reshape(depth, bs, n_mem, d), cache_mem_v.reshape(depth, bs, n_mem, d)))

    mem_out = lambda a: a.reshape(depth, bp, n_mem, MEM_HEADS, MEM_HEAD_DIM)
    return (y_p, y_s[:, :dec_seq], new_p['a'][None], new_s['a'][None], new_p['b'][None], new_s['b'][None],
            new_p['k'][None], new_p['v'][None], new_s['k'][None], new_s['v'][None],
            new_p['r'][None], new_s['r'][None], mem_out(mk), mem_out(mv))
```

```python
import functools
import math

import jax
import jax.numpy as jnp
from jax import lax
from jax.experimental import pallas as pl
from jax.experimental.pallas import tpu as pltpu

F32, BF16 = jnp.float32, jnp.bfloat16

D_MODEL = 1024
NORM_EPS = 1e-6
PAGE_SIZE = 128
CONV_A_WIDTH, CONV_B_WIDTH = 31, 3
SB_HEADS, SB_HEAD_DIM = 16, 64
SB_SCALE = SB_HEAD_DIM ** -0.5
RET_HEADS, RET_DK, RET_DV = 4, 256, 512
ROPE_BASE = 10000.0
MEM_HEADS, MEM_HEAD_DIM = 4, 256
MEM_SCALE = MEM_HEAD_DIM ** -0.5
D_FF = 4 * D_MODEL

CONV_HALO = 32
SAMPLE_ROWS = 8
VMEM_LIMIT = 56 * 1024 * 1024
ROW_TILE = 512
CONV_TILE = 256
SB_BLOCK = 256
SB_HEADS_PER_STEP = 4
SB_MASKED_LOGIT = -1e30
SB_PAGES_PER_STEP = 8
RET_CHUNK = 256

_NT = (((1,), (1,)), ((), ()))


def _dot(a, b):
    return jnp.dot(a.astype(BF16), b, preferred_element_type=F32)


def _dot_nt(a, b):
    return lax.dot_general(a.astype(BF16), b, _NT, preferred_element_type=F32)


def _rms(x, g):
    return x * lax.rsqrt(jnp.mean(x * x, axis=-1, keepdims=True) + NORM_EPS) * g


def _layernorm(x, g, b=None):
    xc = x - jnp.mean(x, axis=-1, keepdims=True)
    y = xc * lax.rsqrt(jnp.mean(xc * xc, axis=-1, keepdims=True) + NORM_EPS) * g
    return y if b is None else y + b


def _silu(x):
    return x * jax.nn.sigmoid(x)


def _pick_tile(n, pref):
    t = min(n, pref)
    while n % t:
        t -= 8
    return t


def _row_call(name, body, m, tm, rows, outs, consts=(), bconsts=(), prows=(), tiles_per_batch=1, outs_t=()):
    in_specs, args = [], []
    for r in rows:
        arr, width, cb = r if isinstance(r, tuple) else (r, r.shape[1], 0)
        in_specs.append(pl.BlockSpec((tm, width), lambda i, cb=cb: (i, cb)))
        args.append(arr)
    for p in prows:
        in_specs.append(pl.BlockSpec((tm, p.shape[1]), lambda i: (i % tiles_per_batch, 0)))
        args.append(p)
    for c in bconsts:
        in_specs.append(pl.BlockSpec((None,) + c.shape[1:],
                                     lambda i, nd=c.ndim: (i // tiles_per_batch,) + (0,) * (nd - 1)))
        args.append(c)
    for c in consts:
        in_specs.append(pl.BlockSpec(c.shape, lambda i, nd=c.ndim: (0,) * nd, pipeline_mode=pl.Buffered(1)))
        args.append(c)
    n_r, n_p, n_b, n_c = len(rows), len(prows), len(bconsts), len(consts)

    def kern(*refs):
        a, b, c, d = n_r, n_r + n_p, n_r + n_p + n_b, n_r + n_p + n_b + n_c
        body(refs[:a], refs[a:b], refs[b:c], refs[c:d], refs[d:])

    t = tm * tiles_per_batch
    fm_spec = lambda f: pl.BlockSpec((None, f, tm), lambda i: (i // tiles_per_batch, 0, i % tiles_per_batch))
    return pl.pallas_call(
        kern, grid=(m // tm,), in_specs=in_specs,
        out_specs=[pl.BlockSpec((tm, w), lambda i: (i, 0)) for w, _ in outs] + [fm_spec(f) for f, _ in outs_t],
        out_shape=[jax.ShapeDtypeStruct((m, w), dt) for w, dt in outs]
        + [jax.ShapeDtypeStruct((m // t, f, t), dt) for f, dt in outs_t],
        compiler_params=pltpu.CompilerParams(dimension_semantics=("parallel",), vmem_limit_bytes=VMEM_LIMIT),
        name=name)(*args)


def _mem_q(x1, g_cross, wq_ref, qn):
    q = _dot(_rms(x1, g_cross), wq_ref[...])
    heads = []
    for hd in range(MEM_HEADS):
        qh = q[:, hd * MEM_HEAD_DIM:(hd + 1) * MEM_HEAD_DIM]
        heads.append(qh * lax.rsqrt(jnp.mean(qh * qh, axis=-1, keepdims=True) + NORM_EPS) * qn)
    return heads


def _mem_attend(qh, mk_h, mv_h):
    s = _dot_nt(qh, mk_h) * MEM_SCALE
    p = jnp.exp(s - jnp.max(s, axis=-1, keepdims=True))
    a = p / jnp.sum(p, axis=-1, keepdims=True)
    return _dot(a, mv_h)


def _mlp(x2, g_mlp, w1_ref, w2_ref):
    h = _rms(x2, g_mlp).astype(BF16)
    acc = x2
    for c in range(D_FF // D_MODEL):
        sl = slice(c * D_MODEL, (c + 1) * D_MODEL)
        u = jnp.dot(h, w1_ref[:, sl], preferred_element_type=F32)
        u = jnp.square(jnp.maximum(u, 0.0))
        acc = acc + _dot(u, w2_ref[sl, :])
    return acc


def _rotary(x, cos, sin):
    half = RET_DK // 2
    parts = []
    for hd in range(RET_HEADS):
        x1 = x[:, hd * RET_DK: hd * RET_DK + half]
        x2 = x[:, hd * RET_DK + half: (hd + 1) * RET_DK]
        parts += [x1 * cos - x2 * sin, x1 * sin + x2 * cos]
    return jnp.concatenate(parts, axis=-1)


def _pre_conv_a(rows, prows, bcs, cs, outs):
    g_mix, w1, b1 = cs
    h = _rms(rows[0][...], g_mix[...])
    a = _dot(h, w1[:, :D_MODEL]) + b1[:, :D_MODEL]
    gate = _dot(h, w1[:, D_MODEL:]) + b1[:, D_MODEL:]
    outs[0][...] = a * jax.nn.sigmoid(gate)


def _pre_conv_b(rows, prows, bcs, cs, outs):
    g_mix, win = cs
    h = _rms(rows[0][...], g_mix[...])
    outs[0][...] = _dot(h, win[:, :D_MODEL])
    outs[1][...] = _dot(h, win[:, D_MODEL:2 * D_MODEL]) * _dot(h, win[:, 2 * D_MODEL:])


def _pre_sb(rows, prows, bcs, cs, outs):
    g_mix, wqkv = cs
    h = _rms(rows[0][...], g_mix[...])
    q = _dot(h, wqkv[:, :D_MODEL])
    k = _dot(h, wqkv[:, D_MODEL:2 * D_MODEL])
    v = _dot(h, wqkv[:, 2 * D_MODEL:])
    outs[0][...] = q
    outs[1][...] = k
    outs[2][...] = v


def _pre_sb_prompt(rows, prows, bcs, cs, outs):
    g_mix, wq, wkt, wvt = cs
    h = _rms(rows[0][...], g_mix[...]).astype(BF16)
    outs[0][...] = (jnp.dot(h, wq[...], preferred_element_type=F32) * SB_SCALE).astype(BF16)
    kt = lax.dot_general(wkt[...], h, _NT, preferred_element_type=F32)
    vt = lax.dot_general(wvt[...], h, _NT, preferred_element_type=F32)
    outs[1][...] = kt
    outs[2][...] = vt
    outs[3][...] = kt.astype(BF16)
    outs[4][...] = vt.astype(BF16)


def _pre_ret(rows, prows, bcs, cs, outs):
    g_mix, w = cs
    cos, sin = prows[0][...], prows[1][...]
    h = _rms(rows[0][...], g_mix[...])
    d = D_MODEL
    outs[0][...] = _rotary(_dot(h, w[:, :d]), cos, sin)
    outs[1][...] = _rotary(_dot(h, w[:, d:2 * d]), cos, sin) * (RET_DK ** -0.5)
    outs[2][...] = _dot(h, w[:, 2 * d:4 * d])
    outs[3][...] = _dot(h, w[:, 4 * d:])


def _post_cross_prompt(rows, prows, bcs, cs, outs):
    x_ref, mix_ref = rows
    mk_ref, mv_ref = bcs
    wo, g_cross, wq, qn, xwo = cs
    x1 = x_ref[...] + _dot(mix_ref[...], wo[...])
    acc = x1
    for hd, qh in enumerate(_mem_q(x1, g_cross[...], wq, qn[...])):
        sl = slice(hd * MEM_HEAD_DIM, (hd + 1) * MEM_HEAD_DIM)
        acc = acc + _dot(_mem_attend(qh, mk_ref[:, sl], mv_ref[:, sl]), xwo[sl, :])
    outs[0][...] = acc


def _post_q_sample(rows, prows, bcs, cs, outs):
    x_ref, mix_ref = rows
    wo, g_cross, wq, qn = cs
    x1 = x_ref[...] + _dot(mix_ref[...], wo[...])
    outs[0][...] = x1
    outs[1][...] = jnp.concatenate(_mem_q(x1, g_cross[...], wq, qn[...]), axis=-1)


def _post_mlp(rows, prows, bcs, cs, outs):
    g_mlp, w1, w2 = cs
    outs[0][...] = _mlp(rows[0][...], g_mlp[...], w1, w2)


def _post_xo_mlp(rows, prows, bcs, cs, outs):
    xwo, g_mlp, w1, w2 = cs
    x2 = rows[0][...] + _dot(rows[1][...], xwo[...])
    outs[0][...] = _mlp(x2, g_mlp[...], w1, w2)


def _mem_kv_call(mem, norm_mem, wk, wv, knorm):
    depth = wk.shape[0]
    m = mem.shape[0]

    def kern(mem_ref, g_ref, wk_ref, wv_ref, kn_ref, k_ref, v_ref, kb_ref, vb_ref):
        mn = _rms(mem_ref[...], g_ref[...])
        k = _dot(mn, wk_ref[...])
        kn = kn_ref[...]
        heads = []
        for hd in range(MEM_HEADS):
            kh = k[:, hd * MEM_HEAD_DIM:(hd + 1) * MEM_HEAD_DIM]
            heads.append(kh * lax.rsqrt(jnp.mean(kh * kh, axis=-1, keepdims=True) + NORM_EPS) * kn)
        k = jnp.concatenate(heads, axis=-1)
        v = _dot(mn, wv_ref[...])
        k_ref[...] = k
        v_ref[...] = v
        kb_ref[...] = k.astype(BF16)
        vb_ref[...] = v.astype(BF16)

    lay = lambda *shape: pl.BlockSpec((None,) + shape, lambda i: (i,) + (0,) * len(shape))
    return pl.pallas_call(
        kern, grid=(depth,),
        in_specs=[pl.BlockSpec(mem.shape, lambda i: (0, 0)), lay(1, D_MODEL), lay(D_MODEL, D_MODEL),
                  lay(D_MODEL, D_MODEL), lay(1, MEM_HEAD_DIM)],
        out_specs=[lay(m, D_MODEL)] * 4,
        out_shape=[jax.ShapeDtypeStruct((depth, m, D_MODEL), F32)] * 2
        + [jax.ShapeDtypeStruct((depth, m, D_MODEL), BF16)] * 2,
        compiler_params=pltpu.CompilerParams(dimension_semantics=("parallel",), vmem_limit_bytes=VMEM_LIMIT),
        name="mem_kv")(mem, norm_mem, wk, wv, knorm)


def _conv_call(name, xin, state, w, epilogue, tm, valid_last, extra_rows=(), consts=()):
    b, t, d = xin.shape
    ktaps = w.shape[0]
    off = CONV_HALO - (ktaps - 1)
    nt = t // tm
    rb = min(16, tm)
    state_pad = jnp.pad(state, ((0, 0), (off, 0), (0, 0)))
    n_e, n_c = len(extra_rows), len(consts)
    shifts = sorted({(off + k) % 8 for k in range(ktaps)} - {0})
    span = tm + CONV_HALO - 8

    def kern(x_ref, st_ref, w_ref, *rest):
        extra, cs = rest[:n_e], rest[n_e:n_e + n_c]
        o_ref, ns_ref, xpad, xs = rest[n_e + n_c:]
        ti = pl.program_id(1)

        @pl.when(ti == 0)
        def _():
            xpad[0:CONV_HALO, :] = st_ref[...]

        xpad[CONV_HALO:CONV_HALO + tm, :] = x_ref[...]
        for si, r in enumerate(shifts):
            xs[si, 0:span, :] = xpad[r:r + span, :]

        def chunk(c, carry):
            r0 = pl.multiple_of(c * rb, rb)
            acc = jnp.zeros((rb, d), F32)
            for k in range(ktaps):
                base, r = (off + k) // 8 * 8, (off + k) % 8
                rows = pl.ds(pl.multiple_of(r0 + base, 8), rb)
                src = xpad[rows, :] if r == 0 else xs[shifts.index(r), rows, :]
                acc = acc + w_ref[k:k + 1, :] * src
            o_ref[pl.ds(r0, rb), :] = epilogue(acc, [e[pl.ds(r0, rb), :] for e in extra], cs)
            return carry

        lax.fori_loop(0, tm // rb, chunk, 0)

        @pl.when(ti == nt - 1)
        def _():
            ns_ref[...] = xpad[valid_last:valid_last + CONV_HALO, :]

        @pl.when(ti < nt - 1)
        def _():
            xpad[0:CONV_HALO, :] = xpad[tm:tm + CONV_HALO, :]

    tile = pl.BlockSpec((None, tm, d), lambda bi, ti: (bi, ti, 0))
    per_seq = pl.BlockSpec((None, CONV_HALO, d), lambda bi, ti: (bi, 0, 0))
    const = lambda c: pl.BlockSpec(c.shape, lambda bi, ti, nd=c.ndim: (0,) * nd)
    out, ns = pl.pallas_call(
        kern, grid=(b, nt),
        in_specs=[tile, per_seq, const(w)] + [tile] * n_e + [const(c) for c in consts],
        out_specs=[tile, per_seq],
        out_shape=[jax.ShapeDtypeStruct((b, t, d), F32), jax.ShapeDtypeStruct((b, CONV_HALO, d), F32)],
        scratch_shapes=[pltpu.VMEM((CONV_HALO + tm, d), F32), pltpu.VMEM((len(shifts), CONV_HALO + tm, d), F32)],
        compiler_params=pltpu.CompilerParams(dimension_semantics=("parallel", "arbitrary"),
                                             vmem_limit_bytes=VMEM_LIMIT),
        name=name)(xin, state_pad, w, *extra_rows, *consts)
    return out, ns[:, off:, :]


def _conv_a_epilogue(acc, extra, cs):
    bdw, ln_g, ln_b = cs
    return _silu(_layernorm(acc + bdw[...], ln_g[...], ln_b[...]))


def _conv_b_epilogue(acc, extra, cs):
    return extra[0] * acc


def _neg_strict_lower_x2(n):
    m = -(jnp.arange(n)[:, None] > jnp.arange(n)[None, :]).astype(BF16)
    return jnp.concatenate([m, m], axis=0)


def _sb_block_weights(z, u2, mask):
    nlk = jnp.maximum(z, 0.0) + jnp.log(1.0 + jnp.exp(-jnp.abs(z)))
    if mask is not None:
        nlk = jnp.where(mask, nlk, 0.0)
    hi = nlk.astype(BF16)
    lo = (nlk - hi.astype(F32)).astype(BF16)
    suffix = jnp.dot(jnp.concatenate([hi, lo], axis=1), u2, preferred_element_type=F32)
    a = jnp.exp(z - nlk + suffix)
    if mask is not None:
        a = jnp.where(mask, a, 0.0)
    return a, jnp.sum(nlk, axis=-1, keepdims=True)


def _sb_prompt_call(qb, kb, vb, bias, blk):
    b, t, d = qb.shape
    nq = t // blk
    hps = SB_HEADS_PER_STEP
    hw = hps * SB_HEAD_DIM
    u2 = _neg_strict_lower_x2(blk)[:blk]

    def kern(bias_ref, q_ref, k_ref, v_ref, u_ref, o_ref, acc_ref, carry_ref, hl_ref, d_ref, a_ref):
        hg, qi = pl.program_id(1), pl.program_id(2)
        tri = lax.broadcasted_iota(jnp.int32, (blk, blk), 1) < lax.broadcasted_iota(jnp.int32, (blk, blk), 0)
        lanes = lambda hh: slice(hh * SB_HEAD_DIM, (hh + 1) * SB_HEAD_DIM)
        cols = lambda kj: pl.ds(pl.multiple_of(kj * blk, blk), blk)

        def logits(hh, kj, mask):
            z = jnp.dot(q_ref[:, lanes(hh)], k_ref[lanes(hh), cols(kj)], preferred_element_type=F32)
            z = z + bias_ref[hg * hps + hh]
            neg_abs = lax.bitcast_convert_type(lax.bitcast_convert_type(z, jnp.uint32) | jnp.uint32(1 << 31), F32)
            nlk = jnp.maximum(z, 0.0) + jnp.log(1.0 + jnp.exp(neg_abs))
            if mask is not None:
                nlk = jnp.where(mask, nlk, 0.0)
            hl_ref[hh] = nlk.astype(BF16)
            c = carry_ref[hh]
            dd = z - nlk + c
            d_ref[hh] = dd if mask is None else jnp.where(mask, dd, SB_MASKED_LOGIT)
            carry_ref[hh] = c - jnp.sum(nlk, axis=-1, keepdims=True)

        def weights(hh):
            suffix = jnp.dot(hl_ref[hh], u_ref[...], preferred_element_type=F32)
            a_ref[hh] = jnp.exp(d_ref[hh] + suffix).astype(BF16)

        def values(hh, kj):
            acc_ref[hh] += lax.dot_general(a_ref[hh], v_ref[lanes(hh), cols(kj)], _NT, preferred_element_type=F32)

        acc_ref[...] = jnp.zeros_like(acc_ref)
        carry_ref[...] = jnp.zeros_like(carry_ref)
        a_ref[...] = jnp.zeros_like(a_ref)
        for hh in range(hps):
            logits(hh, qi, tri)

        def older(it, carry):
            kj = qi - 1 - it
            for hh in range(hps):
                values(hh, jnp.minimum(kj + 2, nq - 1))
            for hh in range(hps):
                weights(hh)
            for hh in range(hps):
                logits(hh, kj, None)
            return carry

        lax.fori_loop(0, qi, older, 0)
        for hh in range(hps):
            values(hh, min(1, nq - 1))
        for hh in range(hps):
            weights(hh)
        for hh in range(hps):
            values(hh, 0)
        o_ref[...] = jnp.concatenate([acc_ref[hh] for hh in range(hps)], axis=-1)

    tile = pl.BlockSpec((None, blk, hw), lambda bi, hg, qi: (bi, qi, hg))
    seq = pl.BlockSpec((None, hw, t), lambda bi, hg, qi: (bi, hg, 0))
    return pl.pallas_call(
        kern, grid=(b, SB_HEADS // hps, nq),
        in_specs=[pl.BlockSpec(memory_space=pltpu.SMEM), tile, seq, seq,
                  pl.BlockSpec(u2.shape, lambda bi, hg, qi: (0, 0))],
        out_specs=tile,
        out_shape=jax.ShapeDtypeStruct((b, t, d), F32),
        scratch_shapes=[pltpu.VMEM((hps, blk, SB_HEAD_DIM), F32), pltpu.VMEM((hps, blk, 1), F32),
                        pltpu.VMEM((hps, blk, blk), BF16), pltpu.VMEM((hps, blk, blk), F32),
                        pltpu.VMEM((hps, blk, blk), BF16)],
        compiler_params=pltpu.CompilerParams(dimension_semantics=("parallel", "parallel", "arbitrary"),
                                             vmem_limit_bytes=VMEM_LIMIT),
        name="sb_prompt")(bias, qb, kb, vb, u2)


def _sb_sample_call(q, k_new, v_new, cache_kt, cache_vt, page_table, bias, n_new):
    bsz, rows, d = q.shape
    n_pages = page_table.shape[1]
    grp = math.gcd(n_pages, SB_PAGES_PER_STEP)
    nr = n_new * SB_HEADS
    q_rep = jnp.repeat(q[:, :n_new], SB_HEADS, axis=1)
    bias_col = jnp.tile(bias, n_new)[:, None]
    u2 = _neg_strict_lower_x2(PAGE_SIZE)
    as_page = lambda x: jnp.swapaxes(jnp.pad(x, ((0, 0), (0, PAGE_SIZE - rows), (0, 0))), 1, 2)

    def kern(pt_ref, q_ref, kn_ref, vn_ref, *rest):
        kp_refs, vp_refs = rest[:grp], rest[grp:2 * grp]
        bias_ref, u_ref, o_ref, qbd, acc, carry = rest[2 * grp:]
        p = pl.program_id(1)
        row = lax.broadcasted_iota(jnp.int32, (nr, d), 0)
        lane = lax.broadcasted_iota(jnp.int32, (nr, d), 1)
        diag = (row % SB_HEADS) == (lane // SB_HEAD_DIM)

        def page(kt_ref, vt_ref, mask):
            z = jnp.dot(qbd[...], kt_ref[...].astype(BF16), preferred_element_type=F32) + bias_ref[...]
            a, tot = _sb_block_weights(z, u_ref[...], mask)
            return _dot_nt(a, vt_ref[...].astype(BF16)), tot

        def accumulate(parts):
            c = carry[...]
            upd = None
            for pv, tot in parts:
                term = jnp.exp(c) * pv
                upd = term if upd is None else upd + term
                c = c - tot
            acc[...] += upd
            carry[...] = c

        @pl.when(p == 0)
        def _():
            qbd[...] = jnp.where(diag, q_ref[...] * SB_SCALE, 0.0).astype(BF16)
            acc[...] = jnp.zeros_like(acc)
            carry[...] = jnp.zeros_like(carry)
            key = lax.broadcasted_iota(jnp.int32, (nr, PAGE_SIZE), 1)
            tq = lax.broadcasted_iota(jnp.int32, (nr, PAGE_SIZE), 0) // SB_HEADS
            accumulate([page(kn_ref, vn_ref, (key < tq) & (key < n_new))])

        zs = [jnp.dot(qbd[...], r[...].astype(BF16), preferred_element_type=F32) for r in kp_refs]
        z = jnp.concatenate(zs, axis=0) + jnp.concatenate([bias_ref[...]] * grp, axis=0)
        a, tot = _sb_block_weights(z, u_ref[...], None)
        part = lambda x, g: x[g * nr:(g + 1) * nr]
        accumulate([(_dot_nt(part(a, g), vp_refs[g][...].astype(BF16)), part(tot, g)) for g in range(grp)])

        @pl.when(p == n_pages // grp - 1)
        def _():
            o_ref[...] = jnp.sum(jnp.where(diag, acc[...], 0.0).reshape(n_new, SB_HEADS, d), axis=1)

    seq = lambda *shape: pl.BlockSpec((None,) + shape, lambda bi, p, pt: (bi, 0, 0))
    pagespec = lambda g: pl.BlockSpec((None, d, PAGE_SIZE),
                                      lambda bi, p, pt: (pt[bi, n_pages - 1 - (p * grp + g)], 0, 0))
    const = lambda c: pl.BlockSpec(c.shape, lambda bi, p, pt: (0, 0))
    pages = [pagespec(g) for g in range(grp)]
    out = pl.pallas_call(
        kern,
        grid_spec=pltpu.PrefetchScalarGridSpec(
            num_scalar_prefetch=1, grid=(bsz, n_pages // grp),
            in_specs=[seq(nr, d), seq(d, PAGE_SIZE), seq(d, PAGE_SIZE)] + pages + pages
            + [const(bias_col), const(u2)],
            out_specs=seq(n_new, d),
            scratch_shapes=[pltpu.VMEM((nr, d), BF16), pltpu.VMEM((nr, d), F32), pltpu.VMEM((nr, 1), F32)]),
        out_shape=jax.ShapeDtypeStruct((bsz, n_new, d), F32),
        compiler_params=pltpu.CompilerParams(dimension_semantics=("parallel", "arbitrary"),
                                             vmem_limit_bytes=VMEM_LIMIT),
        name="sb_sample")(page_table, q_rep, as_page(k_new), as_page(v_new), *([cache_kt] * grp),
                          *([cache_vt] * grp), bias_col, u2)
    return jnp.pad(out, ((0, 0), (0, rows - n_new), (0, 0)))


def _ret_tables(c, valid):
    log_gamma = jnp.log(1.0 - jnp.power(2.0, -5.0 - jnp.arange(RET_HEADS, dtype=F32)))
    idx = jnp.arange(c, dtype=F32)
    diff = idx[:, None] - idx[None, :]
    intra = jnp.where(diff >= 0, jnp.exp(jnp.maximum(diff, 0.0)[None] * log_gamma[:, None, None]), 0.0)
    q_decay = jnp.exp((idx + 1.0)[None, :] * log_gamma[:, None])[..., None]
    k_decay = jnp.where(idx < valid, jnp.exp((valid - 1.0 - idx)[None, :] * log_gamma[:, None]), 0.0)[..., None]
    chunk_decay = jnp.exp(valid * log_gamma)
    return intra, q_decay, k_decay, chunk_decay


def _retention_call(q, k, v, g, s0, gn_g, c, valid):
    b, t, _ = q.shape
    nc = t // c
    intra, q_decay, k_decay, chunk_decay = _ret_tables(c, valid)

    def kern(cd_ref, q_ref, k_ref, v_ref, g_ref, s0_ref, gn_ref, in_ref, qd_ref, kd_ref, y_ref, s_ref):
        ci = pl.program_id(1)

        @pl.when(ci == 0)
        def _():
            s_ref[...] = s0_ref[...]

        for h in range(RET_HEADS):
            qk_l = slice(h * RET_DK, (h + 1) * RET_DK)
            vg_l = slice(h * RET_DV, (h + 1) * RET_DV)
            state = s_ref[h]
            qc, kc = q_ref[:, qk_l].astype(BF16), k_ref[:, qk_l]
            vc = v_ref[:, vg_l].astype(BF16)
            scores = lax.dot_general(qc, kc.astype(BF16), _NT, preferred_element_type=F32) * in_ref[h]
            o = (_dot(scores, vc) + _dot(qc, state.astype(BF16)) * qd_ref[h])
            s_ref[h] = state * cd_ref[h] + _dot((kc * kd_ref[h]).T, vc)
            y_ref[:, vg_l] = _silu(g_ref[:, vg_l]) * _layernorm(o, gn_ref[h])

    qk = pl.BlockSpec((None, c, RET_HEADS * RET_DK), lambda bi, ci: (bi, ci, 0))
    vg = pl.BlockSpec((None, c, RET_HEADS * RET_DV), lambda bi, ci: (bi, ci, 0))
    st = pl.BlockSpec((None, RET_HEADS, RET_DK, RET_DV), lambda bi, ci: (bi, 0, 0, 0))
    const = lambda a: pl.BlockSpec(a.shape, lambda bi, ci, nd=a.ndim: (0,) * nd)
    return pl.pallas_call(
        kern, grid=(b, nc),
        in_specs=[pl.BlockSpec(memory_space=pltpu.SMEM), qk, qk, vg, vg, st, const(gn_g),
                  const(intra), const(q_decay), const(k_decay)],
        out_specs=[vg, st],
        out_shape=[jax.ShapeDtypeStruct((b, t, RET_HEADS * RET_DV), F32),
                   jax.ShapeDtypeStruct((b, RET_HEADS, RET_DK, RET_DV), F32)],
        compiler_params=pltpu.CompilerParams(dimension_semantics=("parallel", "arbitrary"),
                                             vmem_limit_bytes=VMEM_LIMIT),
        name="retention")(chunk_decay, q, k, v, g, s0, gn_g, intra, q_decay, k_decay)


def _mem_attend_sample_call(q, cache_k, cache_v, layer):
    bsz, rows, d = q.shape
    m = cache_k.shape[2]

    def kern(q_ref, k_ref, v_ref, o_ref):
        qv = q_ref[...]
        heads = []
        for hd in range(MEM_HEADS):
            sl = slice(hd * MEM_HEAD_DIM, (hd + 1) * MEM_HEAD_DIM)
            heads.append(_mem_attend(qv[:, sl], k_ref[:, sl].astype(BF16), v_ref[:, sl].astype(BF16)))
        o_ref[...] = jnp.concatenate(heads, axis=-1)

    seq = pl.BlockSpec((None, rows, d), lambda bi: (bi, 0, 0))
    kv = pl.BlockSpec((None, None, m, d), lambda bi: (layer, bi, 0, 0))
    return pl.pallas_call(
        kern, grid=(bsz,), in_specs=[seq, kv, kv], out_specs=seq,
        out_shape=jax.ShapeDtypeStruct((bsz, rows, d), F32),
        compiler_params=pltpu.CompilerParams(dimension_semantics=("parallel",), vmem_limit_bytes=VMEM_LIMIT),
        name="mem_attend_sample")(q, cache_k, cache_v)


def _rope_tables(pos):
    half = RET_DK // 2
    inv_freq = jnp.power(ROPE_BASE, -jnp.arange(0, RET_DK, 2, dtype=F32) / RET_DK)
    ang = pos.astype(F32)[:, None] * inv_freq[None, :]
    return jnp.cos(ang), jnp.sin(ang)


def _trunk(x, t_valid, pos0, p, states, sb_past, mem):
    b, t, d = x.shape
    m = b * t
    tm = _pick_tile(t, ROW_TILE)
    tpb = t // tm
    if mem[0] == 'sample':
        tm, tpb = m, 1
    state_a, state_b, state_r = states
    row = lambda r: r.reshape(m, -1)
    vec = lambda g: g.reshape(1, -1)
    xf = row(x)
    ct = _pick_tile(t, CONV_TILE)
    new = {}
    for i in range(4):
        g_mix = vec(p['norm_mix'][i])
        if i == 0:
            (gl,) = _row_call("pre_conv_a", _pre_conv_a, m, tm, [xf], [(d, F32)],
                              consts=[g_mix, p['cva_w1'], vec(p['cva_b1'])])
            mix, new['a'] = _conv_call("conv_a", gl.reshape(b, t, d), state_a, p['cva_wdw'], _conv_a_epilogue,
                                       ct, t_valid if t_valid < t else ct,
                                       consts=[vec(p['cva_bdw']), vec(p['cva_ln_g']), vec(p['cva_ln_b'])])
            wo = p['cva_w2']
        elif i == 1:
            bg, cu = _row_call("pre_conv_b", _pre_conv_b, m, tm, [xf], [(d, F32)] * 2, consts=[g_mix, p['scb_win']])
            mix, new['b'] = _conv_call("conv_b", cu.reshape(b, t, d), state_b, p['scb_wconv'], _conv_b_epilogue,
                                       ct, t_valid if t_valid < t else ct,
                                       extra_rows=[bg.reshape(b, t, d)])
            wo = p['scb_wout']
        elif i == 2:
            if sb_past is None:
                w = p['sb_wqkv']
                qb, kt, vt, ktb, vtb = _row_call(
                    "pre_sb", _pre_sb_prompt, m, tm, [xf], [(d, BF16)], tiles_per_batch=tpb,
                    consts=[g_mix, w[:, :d], w[:, d:2 * d].T, w[:, 2 * d:].T], outs_t=[(d, F32)] * 2 + [(d, BF16)] * 2)
                mix = _sb_prompt_call(qb.reshape(b, t, d), ktb, vtb, p['sb_bias'], _pick_tile(t, SB_BLOCK))
                token_major = lambda a: jnp.transpose(a.reshape(b, SB_HEADS, SB_HEAD_DIM, t), (0, 3, 1, 2))
                new['k'], new['v'] = token_major(kt), token_major(vt)
            else:
                q, k, v = _row_call("pre_sb", _pre_sb, m, tm, [xf], [(d, F32)] * 3, consts=[g_mix, p['sb_wqkv']])
                cache_k, cache_v, page_table = sb_past
                mix = _sb_sample_call(q.reshape(b, t, d), k.reshape(b, t, d), v.reshape(b, t, d), cache_k, cache_v,
                                      page_table, p['sb_bias'], t_valid)
                per_head = lambda a: a.reshape(b, t, SB_HEADS, SB_HEAD_DIM)[:, :t_valid]
                new['k'], new['v'] = per_head(k), per_head(v)
            wo = p['sb_wo']
        else:
            cos, sin = _rope_tables(pos0 + jnp.arange(t))
            if mem[0] == 'sample':
                cos, sin = jnp.tile(cos, (b, 1)), jnp.tile(sin, (b, 1))
            q, k, v, g = _row_call("pre_ret", _pre_ret, m, tm, [xf], [(d, F32)] * 2 + [(2 * d, F32)] * 2,
                                   consts=[g_mix, p['ret_wqkvg']], prows=[cos, sin], tiles_per_batch=tpb)
            c = _pick_tile(t, RET_CHUNK)
            mix, new['r'] = _retention_call(q.reshape(b, t, d), k.reshape(b, t, d), v.reshape(b, t, 2 * d),
                                            g.reshape(b, t, 2 * d), state_r, p['ret_gn_g'].reshape(RET_HEADS, 1, RET_DV),
                                            c, min(c, t_valid))
            wo = p['ret_wo']
        mix = row(mix)
        g_cross, qn = vec(p['norm_cross'][i]), vec(p['xa_qnorm'][i])
        mlp_consts = [vec(p['norm_mlp'][i]), p['mlp_w1'][i], p['mlp_w2'][i]]
        if mem[0] == 'prompt':
            (x2,) = _row_call("post_cross", _post_cross_prompt, m, tm, [xf, mix], [(d, F32)],
                              consts=[wo, g_cross, p['xa_wq'][i], qn, p['xa_wo'][i]],
                              bconsts=[mem[1][i], mem[2][i]], tiles_per_batch=tpb)
            (xf,) = _row_call("post_mlp", _post_mlp, m, tm, [x2], [(d, F32)], consts=mlp_consts)
        else:
            x1, qm = _row_call("post_q", _post_q_sample, m, tm, [xf, mix], [(d, F32)] * 2,
                               consts=[wo, g_cross, p['xa_wq'][i], qn])
            o = _mem_attend_sample_call(qm.reshape(b, t, d), mem[1], mem[2], i)
            (xf,) = _row_call("post_xo_mlp", _post_xo_mlp, m, tm, [x1, row(o)], [(d, F32)],
                              consts=[p['xa_wo'][i]] + mlp_consts)
    return xf.reshape(b, t, d), new


def kernel(x_prompt, x_sample, mem_prompt, state_conv_a, state_conv_b, cache_sb_k, cache_sb_v, page_table, state_ret, cache_mem_k, cache_mem_v, norm_mix, norm_cross, norm_mem, norm_mlp, xa_wq, xa_wk, xa_wv, xa_wo, xa_qnorm, xa_knorm, mlp_w1, mlp_w2, cva_w1, cva_b1, cva_wdw, cva_bdw, cva_ln_g, cva_ln_b, cva_w2, scb_win, scb_wconv, scb_wout, sb_wqkv, sb_wo, sb_bias, ret_wqkvg, ret_gn_g, ret_wo):
    bp, seq, d = x_prompt.shape
    bs, dec_seq, _ = x_sample.shape
    depth = norm_mix.shape[0]
    assert depth == 4 and d == D_MODEL and dec_seq <= SAMPLE_ROWS
    n_mem = mem_prompt.shape[1]
    past_len = page_table.shape[1] * PAGE_SIZE
    wb = lambda w: w.astype(BF16)
    p = {
        'norm_mix': norm_mix, 'norm_cross': norm_cross, 'norm_mlp': norm_mlp,
        'xa_wq': wb(xa_wq), 'xa_wo': wb(xa_wo), 'xa_qnorm': xa_qnorm, 'mlp_w1': wb(mlp_w1), 'mlp_w2': wb(mlp_w2),
        'cva_w1': wb(cva_w1[0]), 'cva_b1': cva_b1[0], 'cva_wdw': cva_wdw[0], 'cva_bdw': cva_bdw[0],
        'cva_ln_g': cva_ln_g[0], 'cva_ln_b': cva_ln_b[0], 'cva_w2': wb(cva_w2[0]),
        'scb_win': wb(scb_win[0]), 'scb_wconv': scb_wconv[0], 'scb_wout': wb(scb_wout[0]),
        'sb_wqkv': wb(sb_wqkv[0]), 'sb_wo': wb(sb_wo[0]), 'sb_bias': sb_bias[0],
        'ret_wqkvg': wb(ret_wqkvg[0]), 'ret_gn_g': ret_gn_g[0], 'ret_wo': wb(ret_wo[0]),
    }

    mk, mv, mkb, mvb = _mem_kv_call(mem_prompt.reshape(bp * n_mem, d), norm_mem.reshape(depth, 1, d), wb(xa_wk),
                                    wb(xa_wv), xa_knorm.reshape(depth, 1, MEM_HEAD_DIM))
    per_seq = lambda a: a.reshape(depth, bp, n_mem, d)
    zeros = lambda *s: jnp.zeros(s, F32)
    y_p, new_p = _trunk(x_prompt, seq, 0, p,
                        (zeros(bp, CONV_A_WIDTH - 1, d), zeros(bp, CONV_B_WIDTH - 1, d),
                         zeros(bp, RET_HEADS, RET_DK, RET_DV)),
                        None, ('prompt', per_seq(mkb), per_seq(mvb)))

    xs = jnp.pad(x_sample, ((0, 0), (0, SAMPLE_ROWS - dec_seq), (0, 0)))
    pool = cache_sb_k.shape[1]
    feature_major = lambda c: jnp.transpose(c[0], (0, 2, 3, 1)).reshape(pool, d, PAGE_SIZE)
    y_s, new_s = _trunk(xs, dec_seq, past_len, p, (state_conv_a[0], state_conv_b[0], state_ret[0]),
                        (feature_major(cache_sb_k), feature_major(cache_sb_v),
                         page_table),
                        ('sample', cache_mem_k.reshape(depth, bs, n_mem, d), cache_mem_v.reshape(depth, bs, n_mem, d)))

    mem_out = lambda a: a.reshape(depth, bp, n_mem, MEM_HEADS, MEM_HEAD_DIM)
    return (y_p, y_s[:, :dec_seq], new_p['a'][None], new_s['a'][None], new_p['b'][None], new_s['b'][None],
            new_p['k'][None], new_p['v'][None], new_s['k'][None], new_s['v'][None],
            new_p['r'][None], new_s['r'][None], mem_out(mk), mem_out(mv))
```

```python
import functools
import math

import jax
import jax.numpy as jnp
from jax import lax
from jax.experimental import pallas as pl
from jax.experimental.pallas import tpu as pltpu

F32, BF16 = jnp.float32, jnp.bfloat16

D_MODEL = 1024
NORM_EPS = 1e-6
PAGE_SIZE = 128
CONV_A_WIDTH, CONV_B_WIDTH = 31, 3
SB_HEADS, SB_HEAD_DIM = 16, 64
SB_SCALE = SB_HEAD_DIM ** -0.5
RET_HEADS, RET_DK, RET_DV = 4, 256, 512
ROPE_BASE = 10000.0
MEM_HEADS, MEM_HEAD_DIM = 4, 256
MEM_SCALE = MEM_HEAD_DIM ** -0.5
D_FF = 4 * D_MODEL

CONV_HALO = 32
SAMPLE_ROWS = 8
VMEM_LIMIT = 56 * 1024 * 1024
ROW_TILE = 512
CONV_TILE = 256
SB_BLOCK = 256
SB_HEADS_PER_STEP = 4
SB_MASKED_LOGIT = -1e30
SB_PAGES_PER_STEP = 8
RET_CHUNK = 256

_NT = (((1,), (1,)), ((), ()))


def _dot(a, b):
    return jnp.dot(a.astype(BF16), b, preferred_element_type=F32)


def _dot_nt(a, b):
    return lax.dot_general(a.astype(BF16), b, _NT, preferred_element_type=F32)


def _rms(x, g):
    return x * lax.rsqrt(jnp.mean(x * x, axis=-1, keepdims=True) + NORM_EPS) * g


def _layernorm(x, g, b=None):
    xc = x - jnp.mean(x, axis=-1, keepdims=True)
    y = xc * lax.rsqrt(jnp.mean(xc * xc, axis=-1, keepdims=True) + NORM_EPS) * g
    return y if b is None else y + b


def _silu(x):
    return x * jax.nn.sigmoid(x)


def _pick_tile(n, pref):
    t = min(n, pref)
    while n % t:
        t -= 8
    return t


def _row_call(name, body, m, tm, rows, outs, consts=(), bconsts=(), prows=(), tiles_per_batch=1, outs_t=()):
    in_specs, args = [], []
    for r in rows:
        arr, width, cb = r if isinstance(r, tuple) else (r, r.shape[1], 0)
        in_specs.append(pl.BlockSpec((tm, width), lambda i, cb=cb: (i, cb)))
        args.append(arr)
    for p in prows:
        in_specs.append(pl.BlockSpec((tm, p.shape[1]), lambda i: (i % tiles_per_batch, 0)))
        args.append(p)
    for c in bconsts:
        in_specs.append(pl.BlockSpec((None,) + c.shape[1:],
                                     lambda i, nd=c.ndim: (i // tiles_per_batch,) + (0,) * (nd - 1)))
        args.append(c)
    for c in consts:
        in_specs.append(pl.BlockSpec(c.shape, lambda i, nd=c.ndim: (0,) * nd, pipeline_mode=pl.Buffered(1)))
        args.append(c)
    n_r, n_p, n_b, n_c = len(rows), len(prows), len(bconsts), len(consts)

    def kern(*refs):
        a, b, c, d = n_r, n_r + n_p, n_r + n_p + n_b, n_r + n_p + n_b + n_c
        body(refs[:a], refs[a:b], refs[b:c], refs[c:d], refs[d:])

    t = tm * tiles_per_batch
    fm_spec = lambda f: pl.BlockSpec((None, f, tm), lambda i: (i // tiles_per_batch, 0, i % tiles_per_batch))
    return pl.pallas_call(
        kern, grid=(m // tm,), in_specs=in_specs,
        out_specs=[pl.BlockSpec((tm, w), lambda i: (i, 0)) for w, _ in outs] + [fm_spec(f) for f, _ in outs_t],
        out_shape=[jax.ShapeDtypeStruct((m, w), dt) for w, dt in outs]
        + [jax.ShapeDtypeStruct((m // t, f, t), dt) for f, dt in outs_t],
        compiler_params=pltpu.CompilerParams(dimension_semantics=("parallel",), vmem_limit_bytes=VMEM_LIMIT),
        name=name)(*args)


def _mem_q(x1, g_cross, wq_ref, qn):
    q = _dot(_rms(x1, g_cross), wq_ref[...])
    heads = []
    for hd in range(MEM_HEADS):
        qh = q[:, hd * MEM_HEAD_DIM:(hd + 1) * MEM_HEAD_DIM]
        heads.append(qh * lax.rsqrt(jnp.mean(qh * qh, axis=-1, keepdims=True) + NORM_EPS) * qn)
    return heads


def _mem_attend(qh, mk_h, mv_h):
    s = _dot_nt(qh, mk_h) * MEM_SCALE
    p = jnp.exp(s - jnp.max(s, axis=-1, keepdims=True))
    a = p / jnp.sum(p, axis=-1, keepdims=True)
    return _dot(a, mv_h)


def _mlp(x2, g_mlp, w1_ref, w2_ref):
    h = _rms(x2, g_mlp).astype(BF16)
    acc = x2
    for c in range(D_FF // D_MODEL):
        sl = slice(c * D_MODEL, (c + 1) * D_MODEL)
        u = jnp.dot(h, w1_ref[:, sl], preferred_element_type=F32)
        u = jnp.square(jnp.maximum(u, 0.0))
        acc = acc + _dot(u, w2_ref[sl, :])
    return acc


def _rotary(x, cos, sin):
    half = RET_DK // 2
    parts = []
    for hd in range(RET_HEADS):
        x1 = x[:, hd * RET_DK: hd * RET_DK + half]
        x2 = x[:, hd * RET_DK + half: (hd + 1) * RET_DK]
        parts += [x1 * cos - x2 * sin, x1 * sin + x2 * cos]
    return jnp.concatenate(parts, axis=-1)


def _pre_conv_a(rows, prows, bcs, cs, outs):
    g_mix, w1, b1 = cs
    h = _rms(rows[0][...], g_mix[...])
    a = _dot(h, w1[:, :D_MODEL]) + b1[:, :D_MODEL]
    gate = _dot(h, w1[:, D_MODEL:]) + b1[:, D_MODEL:]
    outs[0][...] = a * jax.nn.sigmoid(gate)


def _pre_conv_b(rows, prows, bcs, cs, outs):
    g_mix, win = cs
    h = _rms(rows[0][...], g_mix[...])
    outs[0][...] = _dot(h, win[:, :D_MODEL])
    outs[1][...] = _dot(h, win[:, D_MODEL:2 * D_MODEL]) * _dot(h, win[:, 2 * D_MODEL:])


def _pre_sb(rows, prows, bcs, cs, outs):
    g_mix, wqkv = cs
    h = _rms(rows[0][...], g_mix[...])
    q = _dot(h, wqkv[:, :D_MODEL])
    k = _dot(h, wqkv[:, D_MODEL:2 * D_MODEL])
    v = _dot(h, wqkv[:, 2 * D_MODEL:])
    outs[0][...] = q
    outs[1][...] = k
    outs[2][...] = v


def _pre_sb_prompt(rows, prows, bcs, cs, outs):
    g_mix, wq, wkt, wvt = cs
    h = _rms(rows[0][...], g_mix[...]).astype(BF16)
    outs[0][...] = (jnp.dot(h, wq[...], preferred_element_type=F32) * SB_SCALE).astype(BF16)
    kt = lax.dot_general(wkt[...], h, _NT, preferred_element_type=F32)
    vt = lax.dot_general(wvt[...], h, _NT, preferred_element_type=F32)
    outs[1][...] = kt
    outs[2][...] = vt
    outs[3][...] = kt.astype(BF16)
    outs[4][...] = vt.astype(BF16)


def _pre_ret(rows, prows, bcs, cs, outs):
    g_mix, w = cs
    cos, sin = prows[0][...], prows[1][...]
    h = _rms(rows[0][...], g_mix[...])
    d = D_MODEL
    outs[0][...] = _rotary(_dot(h, w[:, :d]), cos, sin)
    outs[1][...] = _rotary(_dot(h, w[:, d:2 * d]), cos, sin) * (RET_DK ** -0.5)
    outs[2][...] = _dot(h, w[:, 2 * d:4 * d])
    outs[3][...] = _dot(h, w[:, 4 * d:])


def _post_cross_prompt(rows, prows, bcs, cs, outs):
    x_ref, mix_ref = rows
    mk_ref, mv_ref = bcs
    wo, g_cross, wq, qn, xwo = cs
    x1 = x_ref[...] + _dot(mix_ref[...], wo[...])
    acc = x1
    for hd, qh in enumerate(_mem_q(x1, g_cross[...], wq, qn[...])):
        sl = slice(hd * MEM_HEAD_DIM, (hd + 1) * MEM_HEAD_DIM)
        acc = acc + _dot(_mem_attend(qh, mk_ref[:, sl], mv_ref[:, sl]), xwo[sl, :])
    outs[0][...] = acc


def _post_q_sample(rows, prows, bcs, cs, outs):
    x_ref, mix_ref = rows
    wo, g_cross, wq, qn = cs
    x1 = x_ref[...] + _dot(mix_ref[...], wo[...])
    outs[0][...] = x1
    outs[1][...] = jnp.concatenate(_mem_q(x1, g_cross[...], wq, qn[...]), axis=-1)


def _post_mlp(rows, prows, bcs, cs, outs):
    g_mlp, w1, w2 = cs
    outs[0][...] = _mlp(rows[0][...], g_mlp[...], w1, w2)


def _post_xo_mlp(rows, prows, bcs, cs, outs):
    xwo, g_mlp, w1, w2 = cs
    x2 = rows[0][...] + _dot(rows[1][...], xwo[...])
    outs[0][...] = _mlp(x2, g_mlp[...], w1, w2)


def _mem_kv_call(mem, norm_mem, wk, wv, knorm):
    depth = wk.shape[0]
    m = mem.shape[0]

    def kern(mem_ref, g_ref, wk_ref, wv_ref, kn_ref, k_ref, v_ref, kb_ref, vb_ref):
        mn = _rms(mem_ref[...], g_ref[...])
        k = _dot(mn, wk_ref[...])
        kn = kn_ref[...]
        heads = []
        for hd in range(MEM_HEADS):
            kh = k[:, hd * MEM_HEAD_DIM:(hd + 1) * MEM_HEAD_DIM]
            heads.append(kh * lax.rsqrt(jnp.mean(kh * kh, axis=-1, keepdims=True) + NORM_EPS) * kn)
        k = jnp.concatenate(heads, axis=-1)
        v = _dot(mn, wv_ref[...])
        k_ref[...] = k
        v_ref[...] = v
        kb_ref[...] = k.astype(BF16)
        vb_ref[...] = v.astype(BF16)

    lay = lambda *shape: pl.BlockSpec((None,) + shape, lambda i: (i,) + (0,) * len(shape))
    return pl.pallas_call(
        kern, grid=(depth,),
        in_specs=[pl.BlockSpec(mem.shape, lambda i: (0, 0)), lay(1, D_MODEL), lay(D_MODEL, D_MODEL),
                  lay(D_MODEL, D_MODEL), lay(1, MEM_HEAD_DIM)],
        out_specs=[lay(m, D_MODEL)] * 4,
        out_shape=[jax.ShapeDtypeStruct((depth, m, D_MODEL), F32)] * 2
        + [jax.ShapeDtypeStruct((depth, m, D_MODEL), BF16)] * 2,
        compiler_params=pltpu.CompilerParams(dimension_semantics=("parallel",), vmem_limit_bytes=VMEM_LIMIT),
        name="mem_kv")(mem, norm_mem, wk, wv, knorm)


def _conv_call(name, xin, state, w, epilogue, tm, valid_last, extra_rows=(), consts=()):
    b, t, d = xin.shape
    ktaps = w.shape[0]
    off = CONV_HALO - (ktaps - 1)
    nt = t // tm
    rb = min(32, tm)
    state_pad = jnp.pad(state, ((0, 0), (off, 0), (0, 0)))
    n_e, n_c = len(extra_rows), len(consts)
    shifts = sorted({(off + k) % 8 for k in range(ktaps)} - {0})
    span = tm + CONV_HALO - 8

    def kern(x_ref, st_ref, w_ref, *rest):
        extra, cs = rest[:n_e], rest[n_e:n_e + n_c]
        o_ref, ns_ref, xpad, xs = rest[n_e + n_c:]
        ti = pl.program_id(1)

        @pl.when(ti == 0)
        def _():
            xpad[0:CONV_HALO, :] = st_ref[...]

        xpad[CONV_HALO:CONV_HALO + tm, :] = x_ref[...]
        for si, r in enumerate(shifts):
            xs[si, 0:span, :] = xpad[r:r + span, :]

        def chunk(c, carry):
            r0 = pl.multiple_of(c * rb, rb)
            acc = jnp.zeros((rb, d), F32)
            for k in range(ktaps):
                base, r = (off + k) // 8 * 8, (off + k) % 8
                rows = pl.ds(pl.multiple_of(r0 + base, 8), rb)
                src = xpad[rows, :] if r == 0 else xs[shifts.index(r), rows, :]
                acc = acc + w_ref[k:k + 1, :] * src
            o_ref[pl.ds(r0, rb), :] = epilogue(acc, [e[pl.ds(r0, rb), :] for e in extra], cs)
            return carry

        lax.fori_loop(0, tm // rb, chunk, 0)

        @pl.when(ti == nt - 1)
        def _():
            ns_ref[...] = xpad[valid_last:valid_last + CONV_HALO, :]

        @pl.when(ti < nt - 1)
        def _():
            xpad[0:CONV_HALO, :] = xpad[tm:tm + CONV_HALO, :]

    tile = pl.BlockSpec((None, tm, d), lambda bi, ti: (bi, ti, 0))
    per_seq = pl.BlockSpec((None, CONV_HALO, d), lambda bi, ti: (bi, 0, 0))
    const = lambda c: pl.BlockSpec(c.shape, lambda bi, ti, nd=c.ndim: (0,) * nd)
    out, ns = pl.pallas_call(
        kern, grid=(b, nt),
        in_specs=[tile, per_seq, const(w)] + [tile] * n_e + [const(c) for c in consts],
        out_specs=[tile, per_seq],
        out_shape=[jax.ShapeDtypeStruct((b, t, d), F32), jax.ShapeDtypeStruct((b, CONV_HALO, d), F32)],
        scratch_shapes=[pltpu.VMEM((CONV_HALO + tm, d), F32), pltpu.VMEM((len(shifts), CONV_HALO + tm, d), F32)],
        compiler_params=pltpu.CompilerParams(dimension_semantics=("parallel", "arbitrary"),
                                             vmem_limit_bytes=VMEM_LIMIT),
        name=name)(xin, state_pad, w, *extra_rows, *consts)
    return out, ns[:, off:, :]


def _conv_a_epilogue(acc, extra, cs):
    bdw, ln_g, ln_b = cs
    return _silu(_layernorm(acc + bdw[...], ln_g[...], ln_b[...]))


def _conv_b_epilogue(acc, extra, cs):
    return extra[0] * acc


def _neg_strict_lower_x2(n):
    m = -(jnp.arange(n)[:, None] > jnp.arange(n)[None, :]).astype(BF16)
    return jnp.concatenate([m, m], axis=0)


def _sb_block_weights(z, u2, mask):
    nlk = jnp.maximum(z, 0.0) + jnp.log(1.0 + jnp.exp(-jnp.abs(z)))
    if mask is not None:
        nlk = jnp.where(mask, nlk, 0.0)
    hi = nlk.astype(BF16)
    lo = (nlk - hi.astype(F32)).astype(BF16)
    suffix = jnp.dot(jnp.concatenate([hi, lo], axis=1), u2, preferred_element_type=F32)
    a = jnp.exp(z - nlk + suffix)
    if mask is not None:
        a = jnp.where(mask, a, 0.0)
    return a, jnp.sum(nlk, axis=-1, keepdims=True)


def _sb_prompt_call(qb, kb, vb, bias, blk):
    b, t, d = qb.shape
    nq = t // blk
    hps = SB_HEADS_PER_STEP
    hw = hps * SB_HEAD_DIM
    u2 = _neg_strict_lower_x2(blk)[:blk]

    def kern(bias_ref, q_ref, k_ref, v_ref, u_ref, o_ref, acc_ref, carry_ref, hl_ref, d_ref, a_ref):
        hg, qi = pl.program_id(1), pl.program_id(2)
        tri = lax.broadcasted_iota(jnp.int32, (blk, blk), 1) < lax.broadcasted_iota(jnp.int32, (blk, blk), 0)
        lanes = lambda hh: slice(hh * SB_HEAD_DIM, (hh + 1) * SB_HEAD_DIM)
        cols = lambda kj: pl.ds(pl.multiple_of(kj * blk, blk), blk)

        def logits(hh, kj, mask):
            z = jnp.dot(q_ref[:, lanes(hh)], k_ref[lanes(hh), cols(kj)], preferred_element_type=F32)
            z = z + bias_ref[hg * hps + hh]
            neg_abs = lax.bitcast_convert_type(lax.bitcast_convert_type(z, jnp.uint32) | jnp.uint32(1 << 31), F32)
            nlk = jnp.maximum(z, 0.0) + jnp.log(1.0 + jnp.exp(neg_abs))
            if mask is not None:
                nlk = jnp.where(mask, nlk, 0.0)
            hl_ref[hh] = nlk.astype(BF16)
            c = carry_ref[hh]
            dd = z - nlk + c
            d_ref[hh] = dd if mask is None else jnp.where(mask, dd, SB_MASKED_LOGIT)
            carry_ref[hh] = c - jnp.sum(nlk, axis=-1, keepdims=True)

        def weights(hh):
            suffix = jnp.dot(hl_ref[hh], u_ref[...], preferred_element_type=F32)
            a_ref[hh] = jnp.exp(d_ref[hh] + suffix).astype(BF16)

        def values(hh, kj):
            acc_ref[hh] += lax.dot_general(a_ref[hh], v_ref[lanes(hh), cols(kj)], _NT, preferred_element_type=F32)

        acc_ref[...] = jnp.zeros_like(acc_ref)
        carry_ref[...] = jnp.zeros_like(carry_ref)
        a_ref[...] = jnp.zeros_like(a_ref)
        for hh in range(hps):
            logits(hh, qi, tri)

        def older(it, carry):
            kj = qi - 1 - it
            for hh in range(hps):
                values(hh, jnp.minimum(kj + 2, nq - 1))
            for hh in range(hps):
                weights(hh)
            for hh in range(hps):
                logits(hh, kj, None)
            return carry

        lax.fori_loop(0, qi, older, 0)
        for hh in range(hps):
            values(hh, min(1, nq - 1))
        for hh in range(hps):
            weights(hh)
        for hh in range(hps):
            values(hh, 0)
        o_ref[...] = jnp.concatenate([acc_ref[hh] for hh in range(hps)], axis=-1)

    tile = pl.BlockSpec((None, blk, hw), lambda bi, hg, qi: (bi, qi, hg))
    seq = pl.BlockSpec((None, hw, t), lambda bi, hg, qi: (bi, hg, 0))
    return pl.pallas_call(
        kern, grid=(b, SB_HEADS // hps, nq),
        in_specs=[pl.BlockSpec(memory_space=pltpu.SMEM), tile, seq, seq,
                  pl.BlockSpec(u2.shape, lambda bi, hg, qi: (0, 0))],
        out_specs=tile,
        out_shape=jax.ShapeDtypeStruct((b, t, d), F32),
        scratch_shapes=[pltpu.VMEM((hps, blk, SB_HEAD_DIM), F32), pltpu.VMEM((hps, blk, 1), F32),
                        pltpu.VMEM((hps, blk, blk), BF16), pltpu.VMEM((hps, blk, blk), F32),
                        pltpu.VMEM((hps, blk, blk), BF16)],
        compiler_params=pltpu.CompilerParams(dimension_semantics=("parallel", "parallel", "arbitrary"),
                                             vmem_limit_bytes=VMEM_LIMIT),
        name="sb_prompt")(bias, qb, kb, vb, u2)


def _sb_sample_call(q, k_new, v_new, cache_kt, cache_vt, page_table, bias, n_new):
    bsz, rows, d = q.shape
    n_pages = page_table.shape[1]
    grp = math.gcd(n_pages, SB_PAGES_PER_STEP)
    nr = n_new * SB_HEADS
    q_rep = jnp.repeat(q[:, :n_new], SB_HEADS, axis=1)
    bias_col = jnp.tile(bias, n_new)[:, None]
    u2 = _neg_strict_lower_x2(PAGE_SIZE)
    as_page = lambda x: jnp.swapaxes(jnp.pad(x, ((0, 0), (0, PAGE_SIZE - rows), (0, 0))), 1, 2)

    def kern(pt_ref, q_ref, kn_ref, vn_ref, *rest):
        kp_refs, vp_refs = rest[:grp], rest[grp:2 * grp]
        bias_ref, u_ref, o_ref, qbd, acc, carry = rest[2 * grp:]
        p = pl.program_id(1)
        row = lax.broadcasted_iota(jnp.int32, (nr, d), 0)
        lane = lax.broadcasted_iota(jnp.int32, (nr, d), 1)
        diag = (row % SB_HEADS) == (lane // SB_HEAD_DIM)

        def page(kt_ref, vt_ref, mask):
            z = jnp.dot(qbd[...], kt_ref[...].astype(BF16), preferred_element_type=F32) + bias_ref[...]
            a, tot = _sb_block_weights(z, u_ref[...], mask)
            return _dot_nt(a, vt_ref[...].astype(BF16)), tot

        def accumulate(parts):
            c = carry[...]
            upd = None
            for pv, tot in parts:
                term = jnp.exp(c) * pv
                upd = term if upd is None else upd + term
                c = c - tot
            acc[...] += upd
            carry[...] = c

        @pl.when(p == 0)
        def _():
            qbd[...] = jnp.where(diag, q_ref[...] * SB_SCALE, 0.0).astype(BF16)
            acc[...] = jnp.zeros_like(acc)
            carry[...] = jnp.zeros_like(carry)
            key = lax.broadcasted_iota(jnp.int32, (nr, PAGE_SIZE), 1)
            tq = lax.broadcasted_iota(jnp.int32, (nr, PAGE_SIZE), 0) // SB_HEADS
            accumulate([page(kn_ref, vn_ref, (key < tq) & (key < n_new))])

        zs = [jnp.dot(qbd[...], r[...].astype(BF16), preferred_element_type=F32) for r in kp_refs]
        z = jnp.concatenate(zs, axis=0) + jnp.concatenate([bias_ref[...]] * grp, axis=0)
        a, tot = _sb_block_weights(z, u_ref[...], None)
        part = lambda x, g: x[g * nr:(g + 1) * nr]
        accumulate([(_dot_nt(part(a, g), vp_refs[g][...].astype(BF16)), part(tot, g)) for g in range(grp)])

        @pl.when(p == n_pages // grp - 1)
        def _():
            o_ref[...] = jnp.sum(jnp.where(diag, acc[...], 0.0).reshape(n_new, SB_HEADS, d), axis=1)

    seq = lambda *shape: pl.BlockSpec((None,) + shape, lambda bi, p, pt: (bi, 0, 0))
    pagespec = lambda g: pl.BlockSpec((None, d, PAGE_SIZE),
                                      lambda bi, p, pt: (pt[bi, n_pages - 1 - (p * grp + g)], 0, 0))
    const = lambda c: pl.BlockSpec(c.shape, lambda bi, p, pt: (0, 0))
    pages = [pagespec(g) for g in range(grp)]
    out = pl.pallas_call(
        kern,
        grid_spec=pltpu.PrefetchScalarGridSpec(
            num_scalar_prefetch=1, grid=(bsz, n_pages // grp),
            in_specs=[seq(nr, d), seq(d, PAGE_SIZE), seq(d, PAGE_SIZE)] + pages + pages
            + [const(bias_col), const(u2)],
            out_specs=seq(n_new, d),
            scratch_shapes=[pltpu.VMEM((nr, d), BF16), pltpu.VMEM((nr, d), F32), pltpu.VMEM((nr, 1), F32)]),
        out_shape=jax.ShapeDtypeStruct((bsz, n_new, d), F32),
        compiler_params=pltpu.CompilerParams(dimension_semantics=("parallel", "arbitrary"),
                                             vmem_limit_bytes=VMEM_LIMIT),
        name="sb_sample")(page_table, q_rep, as_page(k_new), as_page(v_new), *([cache_kt] * grp),
                          *([cache_vt] * grp), bias_col, u2)
    return jnp.pad(out, ((0, 0), (0, rows - n_new), (0, 0)))


def _ret_tables(c, valid):
    log_gamma = jnp.log(1.0 - jnp.power(2.0, -5.0 - jnp.arange(RET_HEADS, dtype=F32)))
    idx = jnp.arange(c, dtype=F32)
    diff = idx[:, None] - idx[None, :]
    intra = jnp.where(diff >= 0, jnp.exp(jnp.maximum(diff, 0.0)[None] * log_gamma[:, None, None]), 0.0)
    q_decay = jnp.exp((idx + 1.0)[None, :] * log_gamma[:, None])[..., None]
    k_decay = jnp.where(idx < valid, jnp.exp((valid - 1.0 - idx)[None, :] * log_gamma[:, None]), 0.0)[..., None]
    chunk_decay = jnp.exp(valid * log_gamma)
    return intra, q_decay, k_decay, chunk_decay


def _retention_call(q, k, v, g, s0, gn_g, c, valid):
    b, t, _ = q.shape
    nc = t // c
    intra, q_decay, k_decay, chunk_decay = _ret_tables(c, valid)

    def kern(cd_ref, q_ref, k_ref, v_ref, g_ref, s0_ref, gn_ref, in_ref, qd_ref, kd_ref, y_ref, s_ref):
        ci = pl.program_id(1)

        @pl.when(ci == 0)
        def _():
            s_ref[...] = s0_ref[...]

        for h in range(RET_HEADS):
            qk_l = slice(h * RET_DK, (h + 1) * RET_DK)
            vg_l = slice(h * RET_DV, (h + 1) * RET_DV)
            state = s_ref[h]
            qc, kc = q_ref[:, qk_l].astype(BF16), k_ref[:, qk_l]
            vc = v_ref[:, vg_l].astype(BF16)
            scores = lax.dot_general(qc, kc.astype(BF16), _NT, preferred_element_type=F32) * in_ref[h]
            o = (_dot(scores, vc) + _dot(qc, state.astype(BF16)) * qd_ref[h])
            s_ref[h] = state * cd_ref[h] + _dot((kc * kd_ref[h]).T, vc)
            y_ref[:, vg_l] = _silu(g_ref[:, vg_l]) * _layernorm(o, gn_ref[h])

    qk = pl.BlockSpec((None, c, RET_HEADS * RET_DK), lambda bi, ci: (bi, ci, 0))
    vg = pl.BlockSpec((None, c, RET_HEADS * RET_DV), lambda bi, ci: (bi, ci, 0))
    st = pl.BlockSpec((None, RET_HEADS, RET_DK, RET_DV), lambda bi, ci: (bi, 0, 0, 0))
    const = lambda a: pl.BlockSpec(a.shape, lambda bi, ci, nd=a.ndim: (0,) * nd)
    return pl.pallas_call(
        kern, grid=(b, nc),
        in_specs=[pl.BlockSpec(memory_space=pltpu.SMEM), qk, qk, vg, vg, st, const(gn_g),
                  const(intra), const(q_decay), const(k_decay)],
        out_specs=[vg, st],
        out_shape=[jax.ShapeDtypeStruct((b, t, RET_HEADS * RET_DV), F32),
                   jax.ShapeDtypeStruct((b, RET_HEADS, RET_DK, RET_DV), F32)],
        compiler_params=pltpu.CompilerParams(dimension_semantics=("parallel", "arbitrary"),
                                             vmem_limit_bytes=VMEM_LIMIT),
        name="retention")(chunk_decay, q, k, v, g, s0, gn_g, intra, q_decay, k_decay)


def _mem_attend_sample_call(q, cache_k, cache_v, layer):
    bsz, rows, d = q.shape
    m = cache_k.shape[2]

    def kern(q_ref, k_ref, v_ref, o_ref):
        qv = q_ref[...]
        heads = []
        for hd in range(MEM_HEADS):
            sl = slice(hd * MEM_HEAD_DIM, (hd + 1) * MEM_HEAD_DIM)
            heads.append(_mem_attend(qv[:, sl], k_ref[:, sl].astype(BF16), v_ref[:, sl].astype(BF16)))
        o_ref[...] = jnp.concatenate(heads, axis=-1)

    seq = pl.BlockSpec((None, rows, d), lambda bi: (bi, 0, 0))
    kv = pl.BlockSpec((None, None, m, d), lambda bi: (layer, bi, 0, 0))
    return pl.pallas_call(
        kern, grid=(bsz,), in_specs=[seq, kv, kv], out_specs=seq,
        out_shape=jax.ShapeDtypeStruct((bsz, rows, d), F32),
        compiler_params=pltpu.CompilerParams(dimension_semantics=("parallel",), vmem_limit_bytes=VMEM_LIMIT),
        name="mem_attend_sample")(q, cache_k, cache_v)


def _rope_tables(pos):
    half = RET_DK // 2
    inv_freq = jnp.power(ROPE_BASE, -jnp.arange(0, RET_DK, 2, dtype=F32) / RET_DK)
    ang = pos.astype(F32)[:, None] * inv_freq[None, :]
    return jnp.cos(ang), jnp.sin(ang)


def _trunk(x, t_valid, pos0, p, states, sb_past, mem):
    b, t, d = x.shape
    m = b * t
    tm = _pick_tile(t, ROW_TILE)
    tpb = t // tm
    if mem[0] == 'sample':
        tm, tpb = m, 1
    state_a, state_b, state_r = states
    row = lambda r: r.reshape(m, -1)
    vec = lambda g: g.reshape(1, -1)
    xf = row(x)
    ct = _pick_tile(t, CONV_TILE)
    new = {}
    for i in range(4):
        g_mix = vec(p['norm_mix'][i])
        if i == 0:
            (gl,) = _row_call("pre_conv_a", _pre_conv_a, m, tm, [xf], [(d, F32)],
                              consts=[g_mix, p['cva_w1'], vec(p['cva_b1'])])
            mix, new['a'] = _conv_call("conv_a", gl.reshape(b, t, d), state_a, p['cva_wdw'], _conv_a_epilogue,
                                       ct, t_valid if t_valid < t else ct,
                                       consts=[vec(p['cva_bdw']), vec(p['cva_ln_g']), vec(p['cva_ln_b'])])
            wo = p['cva_w2']
        elif i == 1:
            bg, cu = _row_call("pre_conv_b", _pre_conv_b, m, tm, [xf], [(d, F32)] * 2, consts=[g_mix, p['scb_win']])
            mix, new['b'] = _conv_call("conv_b", cu.reshape(b, t, d), state_b, p['scb_wconv'], _conv_b_epilogue,
                                       ct, t_valid if t_valid < t else ct,
                                       extra_rows=[bg.reshape(b, t, d)])
            wo = p['scb_wout']
        elif i == 2:
            if sb_past is None:
                w = p['sb_wqkv']
                qb, kt, vt, ktb, vtb = _row_call(
                    "pre_sb", _pre_sb_prompt, m, tm, [xf], [(d, BF16)], tiles_per_batch=tpb,
                    consts=[g_mix, w[:, :d], w[:, d:2 * d].T, w[:, 2 * d:].T], outs_t=[(d, F32)] * 2 + [(d, BF16)] * 2)
                mix = _sb_prompt_call(qb.reshape(b, t, d), ktb, vtb, p['sb_bias'], _pick_tile(t, SB_BLOCK))
                token_major = lambda a: jnp.transpose(a.reshape(b, SB_HEADS, SB_HEAD_DIM, t), (0, 3, 1, 2))
                new['k'], new['v'] = token_major(kt), token_major(vt)
            else:
                q, k, v = _row_call("pre_sb", _pre_sb, m, tm, [xf], [(d, F32)] * 3, consts=[g_mix, p['sb_wqkv']])
                cache_k, cache_v, page_table = sb_past
                mix = _sb_sample_call(q.reshape(b, t, d), k.reshape(b, t, d), v.reshape(b, t, d), cache_k, cache_v,
                                      page_table, p['sb_bias'], t_valid)
                per_head = lambda a: a.reshape(b, t, SB_HEADS, SB_HEAD_DIM)[:, :t_valid]
                new['k'], new['v'] = per_head(k), per_head(v)
            wo = p['sb_wo']
        else:
            cos, sin = _rope_tables(pos0 + jnp.arange(t))
            if mem[0] == 'sample':
                cos, sin = jnp.tile(cos, (b, 1)), jnp.tile(sin, (b, 1))
            q, k, v, g = _row_call("pre_ret", _pre_ret, m, tm, [xf], [(d, F32)] * 2 + [(2 * d, F32)] * 2,
                                   consts=[g_mix, p['ret_wqkvg']], prows=[cos, sin], tiles_per_batch=tpb)
            c = _pick_tile(t, RET_CHUNK)
            mix, new['r'] = _retention_call(q.reshape(b, t, d), k.reshape(b, t, d), v.reshape(b, t, 2 * d),
                                            g.reshape(b, t, 2 * d), state_r, p['ret_gn_g'].reshape(RET_HEADS, 1, RET_DV),
                                            c, min(c, t_valid))
            wo = p['ret_wo']
        mix = row(mix)
        g_cross, qn = vec(p['norm_cross'][i]), vec(p['xa_qnorm'][i])
        mlp_consts = [vec(p['norm_mlp'][i]), p['mlp_w1'][i], p['mlp_w2'][i]]
        if mem[0] == 'prompt':
            (x2,) = _row_call("post_cross", _post_cross_prompt, m, tm, [xf, mix], [(d, F32)],
                              consts=[wo, g_cross, p['xa_wq'][i], qn, p['xa_wo'][i]],
                              bconsts=[mem[1][i], mem[2][i]], tiles_per_batch=tpb)
            (xf,) = _row_call("post_mlp", _post_mlp, m, tm, [x2], [(d, F32)], consts=mlp_consts)
        else:
            x1, qm = _row_call("post_q", _post_q_sample, m, tm, [xf, mix], [(d, F32)] * 2,
                               consts=[wo, g_cross, p['xa_wq'][i], qn])
            o = _mem_attend_sample_call(qm.reshape(b, t, d), mem[1], mem[2], i)
            (xf,) = _row_call("post_xo_mlp", _post_xo_mlp, m, tm, [x1, row(o)], [(d, F32)],
                              consts=[p['xa_wo'][i]] + mlp_consts)
    return xf.reshape(b, t, d), new


def kernel(x_prompt, x_sample, mem_prompt, state_conv_a, state_conv_b, cache_sb_k, cache_sb_v, page_table, state_ret, cache_mem_k, cache_mem_v, norm_mix, norm_cross, norm_mem, norm_mlp, xa_wq, xa_wk, xa_wv, xa_wo, xa_qnorm, xa_knorm, mlp_w1, mlp_w2, cva_w1, cva_b1, cva_wdw, cva_bdw, cva_ln_g, cva_ln_b, cva_w2, scb_win, scb_wconv, scb_wout, sb_wqkv, sb_wo, sb_bias, ret_wqkvg, ret_gn_g, ret_wo):
    bp, seq, d = x_prompt.shape
    bs, dec_seq, _ = x_sample.shape
    depth = norm_mix.shape[0]
    assert depth == 4 and d == D_MODEL and dec_seq <= SAMPLE_ROWS
    n_mem = mem_prompt.shape[1]
    past_len = page_table.shape[1] * PAGE_SIZE
    wb = lambda w: w.astype(BF16)
    p = {
        'norm_mix': norm_mix, 'norm_cross': norm_cross, 'norm_mlp': norm_mlp,
        'xa_wq': wb(xa_wq), 'xa_wo': wb(xa_wo), 'xa_qnorm': xa_qnorm, 'mlp_w1': wb(mlp_w1), 'mlp_w2': wb(mlp_w2),
        'cva_w1': wb(cva_w1[0]), 'cva_b1': cva_b1[0], 'cva_wdw': cva_wdw[0], 'cva_bdw': cva_bdw[0],
        'cva_ln_g': cva_ln_g[0], 'cva_ln_b': cva_ln_b[0], 'cva_w2': wb(cva_w2[0]),
        'scb_win': wb(scb_win[0]), 'scb_wconv': scb_wconv[0], 'scb_wout': wb(scb_wout[0]),
        'sb_wqkv': wb(sb_wqkv[0]), 'sb_wo': wb(sb_wo[0]), 'sb_bias': sb_bias[0],
        'ret_wqkvg': wb(ret_wqkvg[0]), 'ret_gn_g': ret_gn_g[0], 'ret_wo': wb(ret_wo[0]),
    }

    mk, mv, mkb, mvb = _mem_kv_call(mem_prompt.reshape(bp * n_mem, d), norm_mem.reshape(depth, 1, d), wb(xa_wk),
                                    wb(xa_wv), xa_knorm.reshape(depth, 1, MEM_HEAD_DIM))
    per_seq = lambda a: a.reshape(depth, bp, n_mem, d)
    zeros = lambda *s: jnp.zeros(s, F32)
    y_p, new_p = _trunk(x_prompt, seq, 0, p,
                        (zeros(bp, CONV_A_WIDTH - 1, d), zeros(bp, CONV_B_WIDTH - 1, d),
                         zeros(bp, RET_HEADS, RET_DK, RET_DV)),
                        None, ('prompt', per_seq(mkb), per_seq(mvb)))

    xs = jnp.pad(x_sample, ((0, 0), (0, SAMPLE_ROWS - dec_seq), (0, 0)))
    pool = cache_sb_k.shape[1]
    feature_major = lambda c: jnp.transpose(c[0], (0, 2, 3, 1)).reshape(pool, d, PAGE_SIZE)
    y_s, new_s = _trunk(xs, dec_seq, past_len, p, (state_conv_a[0], state_conv_b[0], state_ret[0]),
                        (feature_major(cache_sb_k), feature_major(cache_sb_v),
                         page_table),
                        ('sample', cache_mem_k.reshape(depth, bs, n_mem, d), cache_mem_v.reshape(depth, bs, n_mem, d)))

    mem_out = lambda a: a.reshape(depth, bp, n_mem, MEM_HEADS, MEM_HEAD_DIM)
    return (y_p, y_s[:, :dec_seq], new_p['a'][None], new_s['a'][None], new_p['b'][None], new_s['b'][None],
            new_p['k'][None], new_p['v'][None], new_s['k'][None], new_s['v'][None],
            new_p['r'][None], new_s['r'][None], mem_out(mk), mem_out(mv))
```

```python
import functools
import math

import jax
import jax.numpy as jnp
from jax import lax
from jax.experimental import pallas as pl
from jax.experimental.pallas import tpu as pltpu

F32, BF16 = jnp.float32, jnp.bfloat16

D_MODEL = 1024
NORM_EPS = 1e-6
PAGE_SIZE = 128
CONV_A_WIDTH, CONV_B_WIDTH = 31, 3
SB_HEADS, SB_HEAD_DIM = 16, 64
SB_SCALE = SB_HEAD_DIM ** -0.5
RET_HEADS, RET_DK, RET_DV = 4, 256, 512
ROPE_BASE = 10000.0
MEM_HEADS, MEM_HEAD_DIM = 4, 256
MEM_SCALE = MEM_HEAD_DIM ** -0.5
D_FF = 4 * D_MODEL

CONV_HALO = 32
SAMPLE_ROWS = 8
VMEM_LIMIT = 56 * 1024 * 1024
ROW_TILE = 512
CONV_TILE = 256
SB_BLOCK = 256
SB_HEADS_PER_STEP = 8
SB_MASKED_LOGIT = -1e30
SB_PAGES_PER_STEP = 8
RET_CHUNK = 256

_NT = (((1,), (1,)), ((), ()))


def _dot(a, b):
    return jnp.dot(a.astype(BF16), b, preferred_element_type=F32)


def _dot_nt(a, b):
    return lax.dot_general(a.astype(BF16), b, _NT, preferred_element_type=F32)


def _rms(x, g):
    return x * lax.rsqrt(jnp.mean(x * x, axis=-1, keepdims=True) + NORM_EPS) * g


def _layernorm(x, g, b=None):
    xc = x - jnp.mean(x, axis=-1, keepdims=True)
    y = xc * lax.rsqrt(jnp.mean(xc * xc, axis=-1, keepdims=True) + NORM_EPS) * g
    return y if b is None else y + b


def _silu(x):
    return x * jax.nn.sigmoid(x)


def _pick_tile(n, pref):
    t = min(n, pref)
    while n % t:
        t -= 8
    return t


def _row_call(name, body, m, tm, rows, outs, consts=(), bconsts=(), prows=(), tiles_per_batch=1, outs_t=()):
    in_specs, args = [], []
    for r in rows:
        arr, width, cb = r if isinstance(r, tuple) else (r, r.shape[1], 0)
        in_specs.append(pl.BlockSpec((tm, width), lambda i, cb=cb: (i, cb)))
        args.append(arr)
    for p in prows:
        in_specs.append(pl.BlockSpec((tm, p.shape[1]), lambda i: (i % tiles_per_batch, 0)))
        args.append(p)
    for c in bconsts:
        in_specs.append(pl.BlockSpec((None,) + c.shape[1:],
                                     lambda i, nd=c.ndim: (i // tiles_per_batch,) + (0,) * (nd - 1)))
        args.append(c)
    for c in consts:
        in_specs.append(pl.BlockSpec(c.shape, lambda i, nd=c.ndim: (0,) * nd, pipeline_mode=pl.Buffered(1)))
        args.append(c)
    n_r, n_p, n_b, n_c = len(rows), len(prows), len(bconsts), len(consts)

    def kern(*refs):
        a, b, c, d = n_r, n_r + n_p, n_r + n_p + n_b, n_r + n_p + n_b + n_c
        body(refs[:a], refs[a:b], refs[b:c], refs[c:d], refs[d:])

    t = tm * tiles_per_batch
    fm_spec = lambda f: pl.BlockSpec((None, f, tm), lambda i: (i // tiles_per_batch, 0, i % tiles_per_batch))
    return pl.pallas_call(
        kern, grid=(m // tm,), in_specs=in_specs,
        out_specs=[pl.BlockSpec((tm, w), lambda i: (i, 0)) for w, _ in outs] + [fm_spec(f) for f, _ in outs_t],
        out_shape=[jax.ShapeDtypeStruct((m, w), dt) for w, dt in outs]
        + [jax.ShapeDtypeStruct((m // t, f, t), dt) for f, dt in outs_t],
        compiler_params=pltpu.CompilerParams(dimension_semantics=("parallel",), vmem_limit_bytes=VMEM_LIMIT),
        name=name)(*args)


def _mem_q(x1, g_cross, wq_ref, qn):
    q = _dot(_rms(x1, g_cross), wq_ref[...])
    heads = []
    for hd in range(MEM_HEADS):
        qh = q[:, hd * MEM_HEAD_DIM:(hd + 1) * MEM_HEAD_DIM]
        heads.append(qh * lax.rsqrt(jnp.mean(qh * qh, axis=-1, keepdims=True) + NORM_EPS) * qn)
    return heads


def _mem_attend(qh, mk_h, mv_h):
    s = _dot_nt(qh, mk_h) * MEM_SCALE
    p = jnp.exp(s - jnp.max(s, axis=-1, keepdims=True))
    a = p / jnp.sum(p, axis=-1, keepdims=True)
    return _dot(a, mv_h)


def _mlp(x2, g_mlp, w1_ref, w2_ref):
    h = _rms(x2, g_mlp).astype(BF16)
    acc = x2
    for c in range(D_FF // D_MODEL):
        sl = slice(c * D_MODEL, (c + 1) * D_MODEL)
        u = jnp.dot(h, w1_ref[:, sl], preferred_element_type=F32)
        u = jnp.square(jnp.maximum(u, 0.0))
        acc = acc + _dot(u, w2_ref[sl, :])
    return acc


def _rotary(x, cos, sin):
    half = RET_DK // 2
    parts = []
    for hd in range(RET_HEADS):
        x1 = x[:, hd * RET_DK: hd * RET_DK + half]
        x2 = x[:, hd * RET_DK + half: (hd + 1) * RET_DK]
        parts += [x1 * cos - x2 * sin, x1 * sin + x2 * cos]
    return jnp.concatenate(parts, axis=-1)


def _pre_conv_a(rows, prows, bcs, cs, outs):
    g_mix, w1, b1 = cs
    h = _rms(rows[0][...], g_mix[...])
    a = _dot(h, w1[:, :D_MODEL]) + b1[:, :D_MODEL]
    gate = _dot(h, w1[:, D_MODEL:]) + b1[:, D_MODEL:]
    outs[0][...] = a * jax.nn.sigmoid(gate)


def _pre_conv_b(rows, prows, bcs, cs, outs):
    g_mix, win = cs
    h = _rms(rows[0][...], g_mix[...])
    outs[0][...] = _dot(h, win[:, :D_MODEL])
    outs[1][...] = _dot(h, win[:, D_MODEL:2 * D_MODEL]) * _dot(h, win[:, 2 * D_MODEL:])


def _pre_sb(rows, prows, bcs, cs, outs):
    g_mix, wqkv = cs
    h = _rms(rows[0][...], g_mix[...])
    q = _dot(h, wqkv[:, :D_MODEL])
    k = _dot(h, wqkv[:, D_MODEL:2 * D_MODEL])
    v = _dot(h, wqkv[:, 2 * D_MODEL:])
    outs[0][...] = q
    outs[1][...] = k
    outs[2][...] = v


def _pre_sb_prompt(rows, prows, bcs, cs, outs):
    g_mix, wq, wkt, wvt = cs
    h = _rms(rows[0][...], g_mix[...]).astype(BF16)
    outs[0][...] = (jnp.dot(h, wq[...], preferred_element_type=F32) * SB_SCALE).astype(BF16)
    kt = lax.dot_general(wkt[...], h, _NT, preferred_element_type=F32)
    vt = lax.dot_general(wvt[...], h, _NT, preferred_element_type=F32)
    outs[1][...] = kt
    outs[2][...] = vt
    outs[3][...] = kt.astype(BF16)
    outs[4][...] = vt.astype(BF16)


def _pre_ret(rows, prows, bcs, cs, outs):
    g_mix, w = cs
    cos, sin = prows[0][...], prows[1][...]
    h = _rms(rows[0][...], g_mix[...])
    d = D_MODEL
    outs[0][...] = _rotary(_dot(h, w[:, :d]), cos, sin)
    outs[1][...] = _rotary(_dot(h, w[:, d:2 * d]), cos, sin) * (RET_DK ** -0.5)
    outs[2][...] = _dot(h, w[:, 2 * d:4 * d])
    outs[3][...] = _dot(h, w[:, 4 * d:])


def _post_cross_prompt(rows, prows, bcs, cs, outs):
    x_ref, mix_ref = rows
    mk_ref, mv_ref = bcs
    wo, g_cross, wq, qn, xwo = cs
    x1 = x_ref[...] + _dot(mix_ref[...], wo[...])
    acc = x1
    for hd, qh in enumerate(_mem_q(x1, g_cross[...], wq, qn[...])):
        sl = slice(hd * MEM_HEAD_DIM, (hd + 1) * MEM_HEAD_DIM)
        acc = acc + _dot(_mem_attend(qh, mk_ref[:, sl], mv_ref[:, sl]), xwo[sl, :])
    outs[0][...] = acc


def _post_q_sample(rows, prows, bcs, cs, outs):
    x_ref, mix_ref = rows
    wo, g_cross, wq, qn = cs
    x1 = x_ref[...] + _dot(mix_ref[...], wo[...])
    outs[0][...] = x1
    outs[1][...] = jnp.concatenate(_mem_q(x1, g_cross[...], wq, qn[...]), axis=-1)


def _post_mlp(rows, prows, bcs, cs, outs):
    g_mlp, w1, w2 = cs
    outs[0][...] = _mlp(rows[0][...], g_mlp[...], w1, w2)


def _post_xo_mlp(rows, prows, bcs, cs, outs):
    xwo, g_mlp, w1, w2 = cs
    x2 = rows[0][...] + _dot(rows[1][...], xwo[...])
    outs[0][...] = _mlp(x2, g_mlp[...], w1, w2)


def _mem_kv_call(mem, norm_mem, wk, wv, knorm):
    depth = wk.shape[0]
    m = mem.shape[0]

    def kern(mem_ref, g_ref, wk_ref, wv_ref, kn_ref, k_ref, v_ref, kb_ref, vb_ref):
        mn = _rms(mem_ref[...], g_ref[...])
        k = _dot(mn, wk_ref[...])
        kn = kn_ref[...]
        heads = []
        for hd in range(MEM_HEADS):
            kh = k[:, hd * MEM_HEAD_DIM:(hd + 1) * MEM_HEAD_DIM]
            heads.append(kh * lax.rsqrt(jnp.mean(kh * kh, axis=-1, keepdims=True) + NORM_EPS) * kn)
        k = jnp.concatenate(heads, axis=-1)
        v = _dot(mn, wv_ref[...])
        k_ref[...] = k
        v_ref[...] = v
        kb_ref[...] = k.astype(BF16)
        vb_ref[...] = v.astype(BF16)

    lay = lambda *shape: pl.BlockSpec((None,) + shape, lambda i: (i,) + (0,) * len(shape))
    return pl.pallas_call(
        kern, grid=(depth,),
        in_specs=[pl.BlockSpec(mem.shape, lambda i: (0, 0)), lay(1, D_MODEL), lay(D_MODEL, D_MODEL),
                  lay(D_MODEL, D_MODEL), lay(1, MEM_HEAD_DIM)],
        out_specs=[lay(m, D_MODEL)] * 4,
        out_shape=[jax.ShapeDtypeStruct((depth, m, D_MODEL), F32)] * 2
        + [jax.ShapeDtypeStruct((depth, m, D_MODEL), BF16)] * 2,
        compiler_params=pltpu.CompilerParams(dimension_semantics=("parallel",), vmem_limit_bytes=VMEM_LIMIT),
        name="mem_kv")(mem, norm_mem, wk, wv, knorm)


def _conv_call(name, xin, state, w, epilogue, tm, valid_last, extra_rows=(), consts=()):
    b, t, d = xin.shape
    ktaps = w.shape[0]
    off = CONV_HALO - (ktaps - 1)
    nt = t // tm
    rb = min(32, tm)
    state_pad = jnp.pad(state, ((0, 0), (off, 0), (0, 0)))
    n_e, n_c = len(extra_rows), len(consts)
    shifts = sorted({(off + k) % 8 for k in range(ktaps)} - {0})
    span = tm + CONV_HALO - 8

    def kern(x_ref, st_ref, w_ref, *rest):
        extra, cs = rest[:n_e], rest[n_e:n_e + n_c]
        o_ref, ns_ref, xpad, xs = rest[n_e + n_c:]
        ti = pl.program_id(1)

        @pl.when(ti == 0)
        def _():
            xpad[0:CONV_HALO, :] = st_ref[...]

        xpad[CONV_HALO:CONV_HALO + tm, :] = x_ref[...]
        for si, r in enumerate(shifts):
            xs[si, 0:span, :] = xpad[r:r + span, :]

        def chunk(c, carry):
            r0 = pl.multiple_of(c * rb, rb)
            acc = jnp.zeros((rb, d), F32)
            for k in range(ktaps):
                base, r = (off + k) // 8 * 8, (off + k) % 8
                rows = pl.ds(pl.multiple_of(r0 + base, 8), rb)
                src = xpad[rows, :] if r == 0 else xs[shifts.index(r), rows, :]
                acc = acc + w_ref[k:k + 1, :] * src
            o_ref[pl.ds(r0, rb), :] = epilogue(acc, [e[pl.ds(r0, rb), :] for e in extra], cs)
            return carry

        lax.fori_loop(0, tm // rb, chunk, 0)

        @pl.when(ti == nt - 1)
        def _():
            ns_ref[...] = xpad[valid_last:valid_last + CONV_HALO, :]

        @pl.when(ti < nt - 1)
        def _():
            xpad[0:CONV_HALO, :] = xpad[tm:tm + CONV_HALO, :]

    tile = pl.BlockSpec((None, tm, d), lambda bi, ti: (bi, ti, 0))
    per_seq = pl.BlockSpec((None, CONV_HALO, d), lambda bi, ti: (bi, 0, 0))
    const = lambda c: pl.BlockSpec(c.shape, lambda bi, ti, nd=c.ndim: (0,) * nd)
    out, ns = pl.pallas_call(
        kern, grid=(b, nt),
        in_specs=[tile, per_seq, const(w)] + [tile] * n_e + [const(c) for c in consts],
        out_specs=[tile, per_seq],
        out_shape=[jax.ShapeDtypeStruct((b, t, d), F32), jax.ShapeDtypeStruct((b, CONV_HALO, d), F32)],
        scratch_shapes=[pltpu.VMEM((CONV_HALO + tm, d), F32), pltpu.VMEM((len(shifts), CONV_HALO + tm, d), F32)],
        compiler_params=pltpu.CompilerParams(dimension_semantics=("parallel", "arbitrary"),
                                             vmem_limit_bytes=VMEM_LIMIT),
        name=name)(xin, state_pad, w, *extra_rows, *consts)
    return out, ns[:, off:, :]


def _conv_a_epilogue(acc, extra, cs):
    bdw, ln_g, ln_b = cs
    return _silu(_layernorm(acc + bdw[...], ln_g[...], ln_b[...]))


def _conv_b_epilogue(acc, extra, cs):
    return extra[0] * acc


def _neg_strict_lower_x2(n):
    m = -(jnp.arange(n)[:, None] > jnp.arange(n)[None, :]).astype(BF16)
    return jnp.concatenate([m, m], axis=0)


def _sb_block_weights(z, u2, mask):
    nlk = jnp.maximum(z, 0.0) + jnp.log(1.0 + jnp.exp(-jnp.abs(z)))
    if mask is not None:
        nlk = jnp.where(mask, nlk, 0.0)
    hi = nlk.astype(BF16)
    lo = (nlk - hi.astype(F32)).astype(BF16)
    suffix = jnp.dot(jnp.concatenate([hi, lo], axis=1), u2, preferred_element_type=F32)
    a = jnp.exp(z - nlk + suffix)
    if mask is not None:
        a = jnp.where(mask, a, 0.0)
    return a, jnp.sum(nlk, axis=-1, keepdims=True)


def _sb_prompt_call(qb, kb, vb, bias, blk):
    b, t, d = qb.shape
    nq = t // blk
    hps = SB_HEADS_PER_STEP
    hw = hps * SB_HEAD_DIM
    u2 = _neg_strict_lower_x2(blk)[:blk]

    def kern(bias_ref, q_ref, k_ref, v_ref, u_ref, o_ref, acc_ref, carry_ref, hl_ref, d_ref, a_ref):
        hg, qi = pl.program_id(1), pl.program_id(2)
        tri = lax.broadcasted_iota(jnp.int32, (blk, blk), 1) < lax.broadcasted_iota(jnp.int32, (blk, blk), 0)
        lanes = lambda hh: slice(hh * SB_HEAD_DIM, (hh + 1) * SB_HEAD_DIM)
        cols = lambda kj: pl.ds(pl.multiple_of(kj * blk, blk), blk)

        def logits(hh, kj, mask):
            z = jnp.dot(q_ref[:, lanes(hh)], k_ref[lanes(hh), cols(kj)], preferred_element_type=F32)
            z = z + bias_ref[hg * hps + hh]
            neg_abs = lax.bitcast_convert_type(lax.bitcast_convert_type(z, jnp.uint32) | jnp.uint32(1 << 31), F32)
            nlk = jnp.maximum(z, 0.0) + jnp.log(1.0 + jnp.exp(neg_abs))
            if mask is not None:
                nlk = jnp.where(mask, nlk, 0.0)
            hl_ref[hh] = nlk.astype(BF16)
            c = carry_ref[hh]
            dd = z - nlk + c
            d_ref[hh] = dd if mask is None else jnp.where(mask, dd, SB_MASKED_LOGIT)
            carry_ref[hh] = c - jnp.sum(nlk, axis=-1, keepdims=True)

        def weights(hh):
            suffix = jnp.dot(hl_ref[hh], u_ref[...], preferred_element_type=F32)
            a_ref[hh] = jnp.exp(d_ref[hh] + suffix).astype(BF16)

        def values(hh, kj):
            acc_ref[hh] += lax.dot_general(a_ref[hh], v_ref[lanes(hh), cols(kj)], _NT, preferred_element_type=F32)

        acc_ref[...] = jnp.zeros_like(acc_ref)
        carry_ref[...] = jnp.zeros_like(carry_ref)
        a_ref[...] = jnp.zeros_like(a_ref)
        for hh in range(hps):
            logits(hh, qi, tri)

        def older(it, carry):
            kj = qi - 1 - it
            for hh in range(hps):
                values(hh, jnp.minimum(kj + 2, nq - 1))
            for hh in range(hps):
                weights(hh)
            for hh in range(hps):
                logits(hh, kj, None)
            return carry

        lax.fori_loop(0, qi, older, 0)
        for hh in range(hps):
            values(hh, min(1, nq - 1))
        for hh in range(hps):
            weights(hh)
        for hh in range(hps):
            values(hh, 0)
        o_ref[...] = jnp.concatenate([acc_ref[hh] for hh in range(hps)], axis=-1)

    tile = pl.BlockSpec((None, blk, hw), lambda bi, hg, qi: (bi, qi, hg))
    seq = pl.BlockSpec((None, hw, t), lambda bi, hg, qi: (bi, hg, 0))
    return pl.pallas_call(
        kern, grid=(b, SB_HEADS // hps, nq),
        in_specs=[pl.BlockSpec(memory_space=pltpu.SMEM), tile, seq, seq,
                  pl.BlockSpec(u2.shape, lambda bi, hg, qi: (0, 0))],
        out_specs=tile,
        out_shape=jax.ShapeDtypeStruct((b, t, d), F32),
        scratch_shapes=[pltpu.VMEM((hps, blk, SB_HEAD_DIM), F32), pltpu.VMEM((hps, blk, 1), F32),
                        pltpu.VMEM((hps, blk, blk), BF16), pltpu.VMEM((hps, blk, blk), F32),
                        pltpu.VMEM((hps, blk, blk), BF16)],
        compiler_params=pltpu.CompilerParams(dimension_semantics=("parallel", "parallel", "arbitrary"),
                                             vmem_limit_bytes=VMEM_LIMIT),
        name="sb_prompt")(bias, qb, kb, vb, u2)


def _sb_sample_call(q, k_new, v_new, cache_kt, cache_vt, page_table, bias, n_new):
    bsz, rows, d = q.shape
    n_pages = page_table.shape[1]
    grp = math.gcd(n_pages, SB_PAGES_PER_STEP)
    nr = n_new * SB_HEADS
    q_rep = jnp.repeat(q[:, :n_new], SB_HEADS, axis=1)
    bias_col = jnp.tile(bias, n_new)[:, None]
    u2 = _neg_strict_lower_x2(PAGE_SIZE)
    as_page = lambda x: jnp.swapaxes(jnp.pad(x, ((0, 0), (0, PAGE_SIZE - rows), (0, 0))), 1, 2)

    def kern(pt_ref, q_ref, kn_ref, vn_ref, *rest):
        kp_refs, vp_refs = rest[:grp], rest[grp:2 * grp]
        bias_ref, u_ref, o_ref, qbd, acc, carry = rest[2 * grp:]
        p = pl.program_id(1)
        row = lax.broadcasted_iota(jnp.int32, (nr, d), 0)
        lane = lax.broadcasted_iota(jnp.int32, (nr, d), 1)
        diag = (row % SB_HEADS) == (lane // SB_HEAD_DIM)

        def page(kt_ref, vt_ref, mask):
            z = jnp.dot(qbd[...], kt_ref[...].astype(BF16), preferred_element_type=F32) + bias_ref[...]
            a, tot = _sb_block_weights(z, u_ref[...], mask)
            return _dot_nt(a, vt_ref[...].astype(BF16)), tot

        def accumulate(parts):
            c = carry[...]
            upd = None
            for pv, tot in parts:
                term = jnp.exp(c) * pv
                upd = term if upd is None else upd + term
                c = c - tot
            acc[...] += upd
            carry[...] = c

        @pl.when(p == 0)
        def _():
            qbd[...] = jnp.where(diag, q_ref[...] * SB_SCALE, 0.0).astype(BF16)
            acc[...] = jnp.zeros_like(acc)
            carry[...] = jnp.zeros_like(carry)
            key = lax.broadcasted_iota(jnp.int32, (nr, PAGE_SIZE), 1)
            tq = lax.broadcasted_iota(jnp.int32, (nr, PAGE_SIZE), 0) // SB_HEADS
            accumulate([page(kn_ref, vn_ref, (key < tq) & (key < n_new))])

        zs = [jnp.dot(qbd[...], r[...].astype(BF16), preferred_element_type=F32) for r in kp_refs]
        z = jnp.concatenate(zs, axis=0) + jnp.concatenate([bias_ref[...]] * grp, axis=0)
        a, tot = _sb_block_weights(z, u_ref[...], None)
        part = lambda x, g: x[g * nr:(g + 1) * nr]
        accumulate([(_dot_nt(part(a, g), vp_refs[g][...].astype(BF16)), part(tot, g)) for g in range(grp)])

        @pl.when(p == n_pages // grp - 1)
        def _():
            o_ref[...] = jnp.sum(jnp.where(diag, acc[...], 0.0).reshape(n_new, SB_HEADS, d), axis=1)

    seq = lambda *shape: pl.BlockSpec((None,) + shape, lambda bi, p, pt: (bi, 0, 0))
    pagespec = lambda g: pl.BlockSpec((None, d, PAGE_SIZE),
                                      lambda bi, p, pt: (pt[bi, n_pages - 1 - (p * grp + g)], 0, 0))
    const = lambda c: pl.BlockSpec(c.shape, lambda bi, p, pt: (0, 0))
    pages = [pagespec(g) for g in range(grp)]
    out = pl.pallas_call(
        kern,
        grid_spec=pltpu.PrefetchScalarGridSpec(
            num_scalar_prefetch=1, grid=(bsz, n_pages // grp),
            in_specs=[seq(nr, d), seq(d, PAGE_SIZE), seq(d, PAGE_SIZE)] + pages + pages
            + [const(bias_col), const(u2)],
            out_specs=seq(n_new, d),
            scratch_shapes=[pltpu.VMEM((nr, d), BF16), pltpu.VMEM((nr, d), F32), pltpu.VMEM((nr, 1), F32)]),
        out_shape=jax.ShapeDtypeStruct((bsz, n_new, d), F32),
        compiler_params=pltpu.CompilerParams(dimension_semantics=("parallel", "arbitrary"),
                                             vmem_limit_bytes=VMEM_LIMIT),
        name="sb_sample")(page_table, q_rep, as_page(k_new), as_page(v_new), *([cache_kt] * grp),
                          *([cache_vt] * grp), bias_col, u2)
    return jnp.pad(out, ((0, 0), (0, rows - n_new), (0, 0)))


def _ret_tables(c, valid):
    log_gamma = jnp.log(1.0 - jnp.power(2.0, -5.0 - jnp.arange(RET_HEADS, dtype=F32)))
    idx = jnp.arange(c, dtype=F32)
    diff = idx[:, None] - idx[None, :]
    intra = jnp.where(diff >= 0, jnp.exp(jnp.maximum(diff, 0.0)[None] * log_gamma[:, None, None]), 0.0)
    q_decay = jnp.exp((idx + 1.0)[None, :] * log_gamma[:, None])[..., None]
    k_decay = jnp.where(idx < valid, jnp.exp((valid - 1.0 - idx)[None, :] * log_gamma[:, None]), 0.0)[..., None]
    chunk_decay = jnp.exp(valid * log_gamma)
    return intra, q_decay, k_decay, chunk_decay


def _retention_call(q, k, v, g, s0, gn_g, c, valid):
    b, t, _ = q.shape
    nc = t // c
    intra, q_decay, k_decay, chunk_decay = _ret_tables(c, valid)

    def kern(cd_ref, q_ref, k_ref, v_ref, g_ref, s0_ref, gn_ref, in_ref, qd_ref, kd_ref, y_ref, s_ref):
        ci = pl.program_id(1)

        @pl.when(ci == 0)
        def _():
            s_ref[...] = s0_ref[...]

        for h in range(RET_HEADS):
            qk_l = slice(h * RET_DK, (h + 1) * RET_DK)
            vg_l = slice(h * RET_DV, (h + 1) * RET_DV)
            state = s_ref[h]
            qc, kc = q_ref[:, qk_l].astype(BF16), k_ref[:, qk_l]
            vc = v_ref[:, vg_l].astype(BF16)
            scores = lax.dot_general(qc, kc.astype(BF16), _NT, preferred_element_type=F32) * in_ref[h]
            o = (_dot(scores, vc) + _dot(qc, state.astype(BF16)) * qd_ref[h])
            s_ref[h] = state * cd_ref[h] + _dot((kc * kd_ref[h]).T, vc)
            y_ref[:, vg_l] = _silu(g_ref[:, vg_l]) * _layernorm(o, gn_ref[h])

    qk = pl.BlockSpec((None, c, RET_HEADS * RET_DK), lambda bi, ci: (bi, ci, 0))
    vg = pl.BlockSpec((None, c, RET_HEADS * RET_DV), lambda bi, ci: (bi, ci, 0))
    st = pl.BlockSpec((None, RET_HEADS, RET_DK, RET_DV), lambda bi, ci: (bi, 0, 0, 0))
    const = lambda a: pl.BlockSpec(a.shape, lambda bi, ci, nd=a.ndim: (0,) * nd)
    return pl.pallas_call(
        kern, grid=(b, nc),
        in_specs=[pl.BlockSpec(memory_space=pltpu.SMEM), qk, qk, vg, vg, st, const(gn_g),
                  const(intra), const(q_decay), const(k_decay)],
        out_specs=[vg, st],
        out_shape=[jax.ShapeDtypeStruct((b, t, RET_HEADS * RET_DV), F32),
                   jax.ShapeDtypeStruct((b, RET_HEADS, RET_DK, RET_DV), F32)],
        compiler_params=pltpu.CompilerParams(dimension_semantics=("parallel", "arbitrary"),
                                             vmem_limit_bytes=VMEM_LIMIT),
        name="retention")(chunk_decay, q, k, v, g, s0, gn_g, intra, q_decay, k_decay)


def _mem_attend_sample_call(q, cache_k, cache_v, layer):
    bsz, rows, d = q.shape
    m = cache_k.shape[2]

    def kern(q_ref, k_ref, v_ref, o_ref):
        qv = q_ref[...]
        heads = []
        for hd in range(MEM_HEADS):
            sl = slice(hd * MEM_HEAD_DIM, (hd + 1) * MEM_HEAD_DIM)
            heads.append(_mem_attend(qv[:, sl], k_ref[:, sl].astype(BF16), v_ref[:, sl].astype(BF16)))
        o_ref[...] = jnp.concatenate(heads, axis=-1)

    seq = pl.BlockSpec((None, rows, d), lambda bi: (bi, 0, 0))
    kv = pl.BlockSpec((None, None, m, d), lambda bi: (layer, bi, 0, 0))
    return pl.pallas_call(
        kern, grid=(bsz,), in_specs=[seq, kv, kv], out_specs=seq,
        out_shape=jax.ShapeDtypeStruct((bsz, rows, d), F32),
        compiler_params=pltpu.CompilerParams(dimension_semantics=("parallel",), vmem_limit_bytes=VMEM_LIMIT),
        name="mem_attend_sample")(q, cache_k, cache_v)


def _rope_tables(pos):
    half = RET_DK // 2
    inv_freq = jnp.power(ROPE_BASE, -jnp.arange(0, RET_DK, 2, dtype=F32) / RET_DK)
    ang = pos.astype(F32)[:, None] * inv_freq[None, :]
    return jnp.cos(ang), jnp.sin(ang)


def _trunk(x, t_valid, pos0, p, states, sb_past, mem):
    b, t, d = x.shape
    m = b * t
    tm = _pick_tile(t, ROW_TILE)
    tpb = t // tm
    if mem[0] == 'sample':
        tm, tpb = m, 1
    state_a, state_b, state_r = states
    row = lambda r: r.reshape(m, -1)
    vec = lambda g: g.reshape(1, -1)
    xf = row(x)
    ct = _pick_tile(t, CONV_TILE)
    new = {}
    for i in range(4):
        g_mix = vec(p['norm_mix'][i])
        if i == 0:
            (gl,) = _row_call("pre_conv_a", _pre_conv_a, m, tm, [xf], [(d, F32)],
                              consts=[g_mix, p['cva_w1'], vec(p['cva_b1'])])
            mix, new['a'] = _conv_call("conv_a", gl.reshape(b, t, d), state_a, p['cva_wdw'], _conv_a_epilogue,
                                       ct, t_valid if t_valid < t else ct,
                                       consts=[vec(p['cva_bdw']), vec(p['cva_ln_g']), vec(p['cva_ln_b'])])
            wo = p['cva_w2']
        elif i == 1:
            bg, cu = _row_call("pre_conv_b", _pre_conv_b, m, tm, [xf], [(d, F32)] * 2, consts=[g_mix, p['scb_win']])
            mix, new['b'] = _conv_call("conv_b", cu.reshape(b, t, d), state_b, p['scb_wconv'], _conv_b_epilogue,
                                       ct, t_valid if t_valid < t else ct,
                                       extra_rows=[bg.reshape(b, t, d)])
            wo = p['scb_wout']
        elif i == 2:
            if sb_past is None:
                w = p['sb_wqkv']
                qb, kt, vt, ktb, vtb = _row_call(
                    "pre_sb", _pre_sb_prompt, m, tm, [xf], [(d, BF16)], tiles_per_batch=tpb,
                    consts=[g_mix, w[:, :d], w[:, d:2 * d].T, w[:, 2 * d:].T], outs_t=[(d, F32)] * 2 + [(d, BF16)] * 2)
                mix = _sb_prompt_call(qb.reshape(b, t, d), ktb, vtb, p['sb_bias'], _pick_tile(t, SB_BLOCK))
                token_major = lambda a: jnp.transpose(a.reshape(b, SB_HEADS, SB_HEAD_DIM, t), (0, 3, 1, 2))
                new['k'], new['v'] = token_major(kt), token_major(vt)
            else:
                q, k, v = _row_call("pre_sb", _pre_sb, m, tm, [xf], [(d, F32)] * 3, consts=[g_mix, p['sb_wqkv']])
                cache_k, cache_v, page_table = sb_past
                mix = _sb_sample_call(q.reshape(b, t, d), k.reshape(b, t, d), v.reshape(b, t, d), cache_k, cache_v,
                                      page_table, p['sb_bias'], t_valid)
                per_head = lambda a: a.reshape(b, t, SB_HEADS, SB_HEAD_DIM)[:, :t_valid]
                new['k'], new['v'] = per_head(k), per_head(v)
            wo = p['sb_wo']
        else:
            cos, sin = _rope_tables(pos0 + jnp.arange(t))
            if mem[0] == 'sample':
                cos, sin = jnp.tile(cos, (b, 1)), jnp.tile(sin, (b, 1))
            q, k, v, g = _row_call("pre_ret", _pre_ret, m, tm, [xf], [(d, F32)] * 2 + [(2 * d, F32)] * 2,
                                   consts=[g_mix, p['ret_wqkvg']], prows=[cos, sin], tiles_per_batch=tpb)
            c = _pick_tile(t, RET_CHUNK)
            mix, new['r'] = _retention_call(q.reshape(b, t, d), k.reshape(b, t, d), v.reshape(b, t, 2 * d),
                                            g.reshape(b, t, 2 * d), state_r, p['ret_gn_g'].reshape(RET_HEADS, 1, RET_DV),
                                            c, min(c, t_valid))
            wo = p['ret_wo']
        mix = row(mix)
        g_cross, qn = vec(p['norm_cross'][i]), vec(p['xa_qnorm'][i])
        mlp_consts = [vec(p['norm_mlp'][i]), p['mlp_w1'][i], p['mlp_w2'][i]]
        if mem[0] == 'prompt':
            (x2,) = _row_call("post_cross", _post_cross_prompt, m, tm, [xf, mix], [(d, F32)],
                              consts=[wo, g_cross, p['xa_wq'][i], qn, p['xa_wo'][i]],
                              bconsts=[mem[1][i], mem[2][i]], tiles_per_batch=tpb)
            (xf,) = _row_call("post_mlp", _post_mlp, m, tm, [x2], [(d, F32)], consts=mlp_consts)
        else:
            x1, qm = _row_call("post_q", _post_q_sample, m, tm, [xf, mix], [(d, F32)] * 2,
                               consts=[wo, g_cross, p['xa_wq'][i], qn])
            o = _mem_attend_sample_call(qm.reshape(b, t, d), mem[1], mem[2], i)
            (xf,) = _row_call("post_xo_mlp", _post_xo_mlp, m, tm, [x1, row(o)], [(d, F32)],
                              consts=[p['xa_wo'][i]] + mlp_consts)
    return xf.reshape(b, t, d), new


def kernel(x_prompt, x_sample, mem_prompt, state_conv_a, state_conv_b, cache_sb_k, cache_sb_v, page_table, state_ret, cache_mem_k, cache_mem_v, norm_mix, norm_cross, norm_mem, norm_mlp, xa_wq, xa_wk, xa_wv, xa_wo, xa_qnorm, xa_knorm, mlp_w1, mlp_w2, cva_w1, cva_b1, cva_wdw, cva_bdw, cva_ln_g, cva_ln_b, cva_w2, scb_win, scb_wconv, scb_wout, sb_wqkv, sb_wo, sb_bias, ret_wqkvg, ret_gn_g, ret_wo):
    bp, seq, d = x_prompt.shape
    bs, dec_seq, _ = x_sample.shape
    depth = norm_mix.shape[0]
    assert depth == 4 and d == D_MODEL and dec_seq <= SAMPLE_ROWS
    n_mem = mem_prompt.shape[1]
    past_len = page_table.shape[1] * PAGE_SIZE
    wb = lambda w: w.astype(BF16)
    p = {
        'norm_mix': norm_mix, 'norm_cross': norm_cross, 'norm_mlp': norm_mlp,
        'xa_wq': wb(xa_wq), 'xa_wo': wb(xa_wo), 'xa_qnorm': xa_qnorm, 'mlp_w1': wb(mlp_w1), 'mlp_w2': wb(mlp_w2),
        'cva_w1': wb(cva_w1[0]), 'cva_b1': cva_b1[0], 'cva_wdw': cva_wdw[0], 'cva_bdw': cva_bdw[0],
        'cva_ln_g': cva_ln_g[0], 'cva_ln_b': cva_ln_b[0], 'cva_w2': wb(cva_w2[0]),
        'scb_win': wb(scb_win[0]), 'scb_wconv': scb_wconv[0], 'scb_wout': wb(scb_wout[0]),
        'sb_wqkv': wb(sb_wqkv[0]), 'sb_wo': wb(sb_wo[0]), 'sb_bias': sb_bias[0],
        'ret_wqkvg': wb(ret_wqkvg[0]), 'ret_gn_g': ret_gn_g[0], 'ret_wo': wb(ret_wo[0]),
    }

    mk, mv, mkb, mvb = _mem_kv_call(mem_prompt.reshape(bp * n_mem, d), norm_mem.reshape(depth, 1, d), wb(xa_wk),
                                    wb(xa_wv), xa_knorm.reshape(depth, 1, MEM_HEAD_DIM))
    per_seq = lambda a: a.reshape(depth, bp, n_mem, d)
    zeros = lambda *s: jnp.zeros(s, F32)
    y_p, new_p = _trunk(x_prompt, seq, 0, p,
                        (zeros(bp, CONV_A_WIDTH - 1, d), zeros(bp, CONV_B_WIDTH - 1, d),
                         zeros(bp, RET_HEADS, RET_DK, RET_DV)),
                        None, ('prompt', per_seq(mkb), per_seq(mvb)))

    xs = jnp.pad(x_sample, ((0, 0), (0, SAMPLE_ROWS - dec_seq), (0, 0)))
    pool = cache_sb_k.shape[1]
    feature_major = lambda c: jnp.transpose(c[0], (0, 2, 3, 1)).reshape(pool, d, PAGE_SIZE)
    y_s, new_s = _trunk(xs, dec_seq, past_len, p, (state_conv_a[0], state_conv_b[0], state_ret[0]),
                        (feature_major(cache_sb_k), feature_major(cache_sb_v),
                         page_table),
                        ('sample', cache_mem_k.reshape(depth, bs, n_mem, d), cache_mem_v.reshape(depth, bs, n_mem, d)))

    mem_out = lambda a: a.reshape(depth, bp, n_mem, MEM_HEADS, MEM_HEAD_DIM)
    return (y_p, y_s[:, :dec_seq], new_p['a'][None], new_s['a'][None], new_p['b'][None], new_s['b'][None],
            new_p['k'][None], new_p['v'][None], new_s['k'][None], new_s['v'][None],
            new_p['r'][None], new_s['r'][None], mem_out(mk), mem_out(mv))
```
